```python
import math
import jax, jax.numpy as jnp
from jax import lax
import numpy as np

D_MODEL = 1024
BATCH = 16
SEQ = 2048
DEPTH = 2

N_EVEN = (DEPTH + 1) // 2
N_ODD = DEPTH // 2
D_FF = 2816
RMS_EPS = 1e-6
LN_EPS = 1e-5
MIX_WIDTH = D_MODEL
CONV_CH = MIX_WIDTH // 2
CONV_TAPS = 31
SSM_WIDTH = MIX_WIDTH - CONV_CH
SSM_GROUP = 16
SSM_GROUPS = SSM_WIDTH // SSM_GROUP
SSM_STATE = 64
DT_MIN = 1e-3
DT_MAX = 1e-1
IN_WIDTH = 2 * CONV_CH + SSM_WIDTH
N_HEADS = 8
HEAD_DIM = D_MODEL // N_HEADS
MOBA_BLOCK = 256
MOBA_TOPK = 3
Q_CHUNK = 8

kernel_name = "hybrid_conv_s5_moba_macaron"


def rms_norm(x, g):
    xf = x.astype(jnp.float32)
    y = xf * lax.rsqrt(jnp.mean(xf * xf, axis=-1, keepdims=True) + RMS_EPS)
    return (y * g.astype(jnp.float32)).astype(x.dtype)


def swiglu(h, w1, w3, w2):
    return (jax.nn.silu(h @ w1) * (h @ w3)) @ w2


def conformer_conv(a, g, conv_w, conv_b, ln_g, ln_b):
    v = a * jax.nn.sigmoid(g)
    y = lax.conv_general_dilated(
        v, conv_w[:, None, :].astype(v.dtype), window_strides=(1,),
        padding=[(CONV_TAPS - 1, 0)], dimension_numbers=("NWC", "WIO", "NWC"),
        feature_group_count=CONV_CH) + conv_b
    yf = y.astype(jnp.float32)
    mu = jnp.mean(yf, axis=-1, keepdims=True)
    var = jnp.mean(jnp.square(yf - mu), axis=-1, keepdims=True)
    yn = (yf - mu) * lax.rsqrt(var + LN_EPS) * ln_g.astype(jnp.float32) + ln_b.astype(jnp.float32)
    return jax.nn.silu(yn).astype(a.dtype)


def _complex_linear_combine(e1, e2):
    a1r, a1i, b1r, b1i = e1
    a2r, a2i, b2r, b2i = e2
    ar = a1r * a2r - a1i * a2i
    ai = a1r * a2i + a1i * a2r
    br = a2r * b1r - a2i * b1i + b2r
    bi = a2r * b1i + a2i * b1r + b2i
    return (ar, ai, br, bi)


def s5_ssm(u, a_re, a_im, b_re, b_im, c_re, c_im, d, log_dt, glu_w, glu_b):
    f32 = jnp.float32
    bsz, L, _ = u.shape
    a_re, a_im = a_re.astype(f32), a_im.astype(f32)
    b_re, b_im = b_re.astype(f32), b_im.astype(f32)
    c_re, c_im = c_re.astype(f32), c_im.astype(f32)
    dt = jnp.exp(log_dt.astype(f32))[:, None]
    mag = jnp.exp(dt * a_re)
    ang = dt * a_im
    abar_re = mag * jnp.cos(ang)
    abar_im = mag * jnp.sin(ang)
    den = a_re * a_re + a_im * a_im
    nr = abar_re - 1.0
    ni = abar_im
    q_re = (nr * a_re + ni * a_im) / den
    q_im = (ni * a_re - nr * a_im) / den
    bbar_re = q_re[..., None] * b_re - q_im[..., None] * b_im
    bbar_im = q_re[..., None] * b_im + q_im[..., None] * b_re
    uf = u.astype(f32)
    ug = uf.reshape(bsz, L, SSM_GROUPS, SSM_GROUP)
    bu_re = jnp.einsum("blgh,gph->blgp", ug, bbar_re)
    bu_im = jnp.einsum("blgh,gph->blgp", ug, bbar_im)
    a_seq_re = jnp.broadcast_to(abar_re[None, None], (1, L, SSM_GROUPS, SSM_STATE))
    a_seq_im = jnp.broadcast_to(abar_im[None, None], (1, L, SSM_GROUPS, SSM_STATE))
    _, _, x_re, x_im = lax.associative_scan(
        _complex_linear_combine, (a_seq_re, a_seq_im, bu_re, bu_im), axis=1)
    y = jnp.einsum("blgp,ghp->blgh", x_re, c_re) - jnp.einsum("blgp,ghp->blgh", x_im, c_im)
    y = y.reshape(bsz, L, SSM_WIDTH) + d.astype(f32) * uf
    y = jax.nn.gelu(y)
    y = y * jax.nn.sigmoid(y @ glu_w.astype(f32) + glu_b.astype(f32))
    return y.astype(u.dtype)


def conv_ssm_mixer(h, w_in, conv_w, conv_b, ln_g, ln_b, a_re, a_im, b_re, b_im,
                   c_re, c_im, d, log_dt, glu_w, glu_b, w_out):
    p = h @ w_in
    a = p[..., :CONV_CH]
    g = p[..., CONV_CH:2 * CONV_CH]
    u = p[..., 2 * CONV_CH:]
    y_conv = conformer_conv(a, g, conv_w, conv_b, ln_g, ln_b)
    y_ssm = s5_ssm(u, a_re, a_im, b_re, b_im, c_re, c_im, d, log_dt, glu_w, glu_b)
    return jnp.concatenate([y_conv, y_ssm], axis=-1) @ w_out


def moba_attention(h, w_qkv, w_o):
    f32 = jnp.float32
    bsz, L, _ = h.shape
    qkv = (h @ w_qkv).reshape(bsz, L, 3, N_HEADS, HEAD_DIM)
    q = qkv[:, :, 0].transpose(0, 2, 1, 3)
    k = qkv[:, :, 1].transpose(0, 2, 1, 3)
    v = qkv[:, :, 2].transpose(0, 2, 1, 3)
    n_blk = -(-L // MOBA_BLOCK)
    pad = n_blk * MOBA_BLOCK - L
    kp = jnp.pad(k, ((0, 0), (0, 0), (0, pad), (0, 0)))
    vp = jnp.pad(v, ((0, 0), (0, 0), (0, pad), (0, 0)))
    kb = kp.reshape(bsz, N_HEADS, n_blk, MOBA_BLOCK, HEAD_DIM)
    vb = vp.reshape(bsz, N_HEADS, n_blk, MOBA_BLOCK, HEAD_DIM)
    k_mean = jnp.mean(kb.astype(f32), axis=3)
    gate = jnp.einsum("bhqd,bhnd->bhqn", q.astype(f32), k_mean)
    q_blk = jnp.arange(L) // MOBA_BLOCK
    fully_past = jnp.arange(n_blk)[None, :] < q_blk[:, None]
    gate = jnp.where(fully_past, gate, -jnp.inf)
    n_sel = min(MOBA_TOPK, n_blk)
    g_val, g_idx = lax.top_k(gate, n_sel)
    g_ok = jnp.isfinite(g_val)
    scale = HEAD_DIM ** -0.5
    b_ix = jnp.arange(bsz)[:, None, None, None]
    h_ix = jnp.arange(N_HEADS)[None, :, None, None]

    def chunk(c):
        start = c * Q_CHUNK
        qc = lax.dynamic_slice_in_dim(q, start, Q_CHUNK, axis=2)
        idx = lax.dynamic_slice_in_dim(g_idx, start, Q_CHUNK, axis=2)
        ok = lax.dynamic_slice_in_dim(g_ok, start, Q_CHUNK, axis=2)
        k_sel = kb[b_ix, h_ix, idx]
        v_sel = vb[b_ix, h_ix, idx]
        s_sel = jnp.einsum("bhqd,bhqskd->bhqsk", qc, k_sel).astype(f32) * scale
        s_sel = jnp.where(ok[..., None], s_sel, -jnp.inf)
        s_sel = s_sel.reshape(bsz, N_HEADS, Q_CHUNK, n_sel * MOBA_BLOCK)
        own_start = (start // MOBA_BLOCK) * MOBA_BLOCK
        k_own = lax.dynamic_slice_in_dim(kp, own_start, MOBA_BLOCK, axis=2)
        v_own = lax.dynamic_slice_in_dim(vp, own_start, MOBA_BLOCK, axis=2)
        s_own = jnp.einsum("bhqd,bhkd->bhqk", qc, k_own).astype(f32) * scale
        causal = (own_start + jnp.arange(MOBA_BLOCK))[None, :] <= (start + jnp.arange(Q_CHUNK))[:, None]
        s_own = jnp.where(causal, s_own, -jnp.inf)
        p = jax.nn.softmax(jnp.concatenate([s_sel, s_own], axis=-1), axis=-1).astype(v.dtype)
        p_sel = p[..., :n_sel * MOBA_BLOCK].reshape(bsz, N_HEADS, Q_CHUNK, n_sel, MOBA_BLOCK)
        p_own = p[..., n_sel * MOBA_BLOCK:]
        return (jnp.einsum("bhqsk,bhqskd->bhqd", p_sel, v_sel)
                + jnp.einsum("bhqk,bhkd->bhqd", p_own, v_own))

    out = lax.map(chunk, jnp.arange(L // Q_CHUNK))
    out = out.transpose(1, 0, 3, 2, 4).reshape(bsz, L, N_HEADS * HEAD_DIM)
    return out @ w_o


def setup_inputs(seed: int = 0) -> dict:
    key = jax.random.key(seed)
    ks = jax.random.split(key, 26)
    f32 = jnp.float32

    def nrm(k, shape, scale):
        return jax.random.normal(k, shape, f32) * scale

    x = nrm(ks[0], (BATCH, SEQ, D_MODEL), 1.0)
    ffn_norm = 1.0 + nrm(ks[1], (DEPTH, 2, D_MODEL), 0.02)
    ffn_w1 = nrm(ks[2], (DEPTH, 2, D_MODEL, D_FF), D_MODEL ** -0.5)
    ffn_w3 = nrm(ks[3], (DEPTH, 2, D_MODEL, D_FF), D_MODEL ** -0.5)
    ffn_w2 = nrm(ks[4], (DEPTH, 2, D_FF, D_MODEL), D_FF ** -0.5)
    mix_norm = 1.0 + nrm(ks[5], (DEPTH, D_MODEL), 0.02)
    ab_w_in = nrm(ks[6], (N_EVEN, D_MODEL, IN_WIDTH), D_MODEL ** -0.5)
    conv_w = nrm(ks[7], (N_EVEN, CONV_TAPS, CONV_CH), CONV_TAPS ** -0.5)
    conv_b = nrm(ks[8], (N_EVEN, CONV_CH), 0.02)
    conv_ln_g = 1.0 + nrm(ks[9], (N_EVEN, CONV_CH), 0.02)
    conv_ln_b = nrm(ks[10], (N_EVEN, CONV_CH), 0.02)
    n_idx = jnp.arange(SSM_STATE, dtype=f32)
    ssm_a_re = -0.5 + nrm(ks[11], (N_EVEN, SSM_GROUPS, SSM_STATE), 0.01)
    ssm_a_im = math.pi * n_idx + nrm(ks[12], (N_EVEN, SSM_GROUPS, SSM_STATE), 0.01)
    ssm_b_re = nrm(ks[13], (N_EVEN, SSM_GROUPS, SSM_STATE, SSM_GROUP), (2 * SSM_GROUP) ** -0.5)
    ssm_b_im = nrm(ks[14], (N_EVEN, SSM_GROUPS, SSM_STATE, SSM_GROUP), (2 * SSM_GROUP) ** -0.5)
    ssm_c_re = nrm(ks[15], (N_EVEN, SSM_GROUPS, SSM_GROUP, SSM_STATE), (2 * SSM_STATE) ** -0.5)
    ssm_c_im = nrm(ks[16], (N_EVEN, SSM_GROUPS, SSM_GROUP, SSM_STATE), (2 * SSM_STATE) ** -0.5)
    ssm_d = nrm(ks[17], (N_EVEN, SSM_WIDTH), 1.0)
    ssm_log_dt = jax.random.uniform(ks[18], (N_EVEN, SSM_GROUPS), f32,
                                    minval=math.log(DT_MIN), maxval=math.log(DT_MAX))
    ssm_glu_w = nrm(ks[19], (N_EVEN, SSM_WIDTH, SSM_WIDTH), SSM_WIDTH ** -0.5)
    ssm_glu_b = nrm(ks[20], (N_EVEN, SSM_WIDTH), 0.02)
    ab_w_out = nrm(ks[21], (N_EVEN, MIX_WIDTH, D_MODEL), MIX_WIDTH ** -0.5)
    attn_w_qkv = nrm(ks[22], (N_ODD, D_MODEL, 3 * N_HEADS * HEAD_DIM), D_MODEL ** -0.5)
    attn_w_o = nrm(ks[23], (N_ODD, N_HEADS * HEAD_DIM, D_MODEL), (N_HEADS * HEAD_DIM) ** -0.5)
    final_norm = 1.0 + nrm(ks[24], (D_MODEL,), 0.02)
    return {"x": x, "ffn_norm": ffn_norm, "ffn_w1": ffn_w1, "ffn_w3": ffn_w3, "ffn_w2": ffn_w2,
            "mix_norm": mix_norm, "ab_w_in": ab_w_in, "conv_w": conv_w, "conv_b": conv_b,
            "conv_ln_g": conv_ln_g, "conv_ln_b": conv_ln_b, "ssm_a_re": ssm_a_re,
            "ssm_a_im": ssm_a_im, "ssm_b_re": ssm_b_re, "ssm_b_im": ssm_b_im,
            "ssm_c_re": ssm_c_re, "ssm_c_im": ssm_c_im, "ssm_d": ssm_d,
            "ssm_log_dt": ssm_log_dt, "ssm_glu_w": ssm_glu_w, "ssm_glu_b": ssm_glu_b,
            "ab_w_out": ab_w_out, "attn_w_qkv": attn_w_qkv, "attn_w_o": attn_w_o,
            "final_norm": final_norm}


def reference(x, ffn_norm, ffn_w1, ffn_w3, ffn_w2, mix_norm, ab_w_in, conv_w, conv_b,
              conv_ln_g, conv_ln_b, ssm_a_re, ssm_a_im, ssm_b_re, ssm_b_im, ssm_c_re,
              ssm_c_im, ssm_d, ssm_log_dt, ssm_glu_w, ssm_glu_b, ab_w_out, attn_w_qkv,
              attn_w_o, final_norm):
    for l in range(DEPTH):
        h = rms_norm(x, ffn_norm[l, 0])
        x = x + 0.5 * swiglu(h, ffn_w1[l, 0], ffn_w3[l, 0], ffn_w2[l, 0])
        h = rms_norm(x, mix_norm[l])
        if l % 2 == 0:
            e = l // 2
            x = x + conv_ssm_mixer(h, ab_w_in[e], conv_w[e], conv_b[e], conv_ln_g[e], conv_ln_b[e],
                                   ssm_a_re[e], ssm_a_im[e], ssm_b_re[e], ssm_b_im[e],
                                   ssm_c_re[e], ssm_c_im[e], ssm_d[e], ssm_log_dt[e],
                                   ssm_glu_w[e], ssm_glu_b[e], ab_w_out[e])
        else:
            o = l // 2
            x = x + moba_attention(h, attn_w_qkv[o], attn_w_o[o])
        h = rms_norm(x, ffn_norm[l, 1])
        x = x + 0.5 * swiglu(h, ffn_w1[l, 1], ffn_w3[l, 1], ffn_w2[l, 1])
    return rms_norm(x, final_norm)
```

```python
import functools
import math

import jax
import jax.numpy as jnp
from jax import lax
from jax.experimental import pallas as pl
from jax.experimental.pallas import tpu as pltpu

F32 = jnp.float32
BF16 = jnp.bfloat16

D_MODEL = 1024
D_FF = 2816
RMS_EPS = 1e-6
LN_EPS = 1e-5
CONV_CH = 512
CONV_TAPS = 31
SSM_WIDTH = 512
SSM_GROUP = 16
SSM_GROUPS = SSM_WIDTH // SSM_GROUP
SSM_STATE = 64
IN_WIDTH = 2 * CONV_CH + SSM_WIDTH
N_HEADS = 8
HEAD_DIM = 128
MOBA_BLOCK = 256
MOBA_TOPK = 3

V7X_LANES = 128
V7X_SUBLANES = 8
V7X_VMEM_BYTES = 64 * 1024 * 1024

ROW_TILE = 512
FF_CHUNK = 1408
CONV_HALO = 32
S5_CHUNK = 16
S5_LANES = S5_CHUNK * SSM_GROUP
MASK_VALUE = -1e30
SOFTMAX_EXP2_SCALE = HEAD_DIM ** -0.5 * math.log2(math.e)


def _vmem_limit(n_bytes):
    return int(min(n_bytes, V7X_VMEM_BYTES - 8 * 1024 * 1024))


def _params(n_grid, vmem_bytes):
    return pltpu.CompilerParams(
        dimension_semantics=("arbitrary",) * n_grid, vmem_limit_bytes=_vmem_limit(vmem_bytes))


def _resident(shape):
    nd = len(shape)
    return pl.BlockSpec(shape, lambda *_: (0,) * nd, pipeline_mode=pl.Buffered(1))


def _rms(x, g):
    return x * lax.rsqrt(jnp.mean(x * x, axis=-1, keepdims=True) + RMS_EPS) * g


def _dot(a, b, **kw):
    return jnp.dot(a, b, preferred_element_type=F32, **kw)


def _dot_nt(a, b):
    return lax.dot_general(a, b, (((1,), (1,)), ((), ())), preferred_element_type=F32)


def _ffn_body(*refs, final):
    if final:
        x_ref, g_ref, w1_ref, w3_ref, w2_ref, gf_ref, o_ref = refs
    else:
        x_ref, g_ref, w1_ref, w3_ref, w2_ref, o_ref = refs
    x = x_ref[...]
    h = _rms(x, g_ref[...]).astype(BF16)
    acc = None
    for c0 in range(0, D_FF, FF_CHUNK):
        a = _dot(h, w1_ref[:, c0:c0 + FF_CHUNK])
        b = _dot(h, w3_ref[:, c0:c0 + FF_CHUNK])
        act = (a * jax.nn.sigmoid(a) * b).astype(BF16)
        part = _dot(act, w2_ref[c0:c0 + FF_CHUNK, :])
        acc = part if acc is None else acc + part
    y = x + 0.5 * acc
    if final:
        y = _rms(y, gf_ref[...])
    o_ref[...] = y


def _ffn(x, g, w1, w3, w2, final_g=None):
    t = x.shape[0]
    row = pl.BlockSpec((ROW_TILE, D_MODEL), lambda i: (i, 0))
    in_specs = [row, _resident((1, D_MODEL)), _resident((D_MODEL, D_FF)),
                _resident((D_MODEL, D_FF)), _resident((D_FF, D_MODEL))]
    args = [x, g.reshape(1, D_MODEL), w1.astype(BF16), w3.astype(BF16), w2.astype(BF16)]
    if final_g is not None:
        in_specs.append(_resident((1, D_MODEL)))
        args.append(final_g.reshape(1, D_MODEL))
    return pl.pallas_call(
        functools.partial(_ffn_body, final=final_g is not None),
        grid=(t // ROW_TILE,), in_specs=in_specs, out_specs=row,
        out_shape=jax.ShapeDtypeStruct((t, D_MODEL), F32),
        compiler_params=_params(1, 48 * 1024 * 1024), name="ffn")(*args)


def _mix_in_body(x_ref, g_ref, win_ref, cw_ref, cb_ref, lng_ref, lnb_ref, yc_ref, u_ref, vpad_ref):
    tl = x_ref.shape[0]

    @pl.when(pl.program_id(1) == 0)
    def _():
        vpad_ref[0:CONV_HALO, :] = jnp.zeros((CONV_HALO, CONV_CH), F32)

    h = _rms(x_ref[...], g_ref[...]).astype(BF16)
    p = _dot(h, win_ref[...])
    u_ref[...] = p[:, 2 * CONV_CH:]
    vpad_ref[CONV_HALO:CONV_HALO + tl, :] = p[:, :CONV_CH] * jax.nn.sigmoid(p[:, CONV_CH:2 * CONV_CH])
    first = CONV_HALO - (CONV_TAPS - 1)
    acc = jnp.zeros((tl, CONV_CH), F32)
    for j in range(CONV_TAPS):
        acc = acc + cw_ref[j:j + 1, :] * vpad_ref[first + j:first + j + tl, :]
    vpad_ref[0:CONV_HALO, :] = vpad_ref[tl:tl + CONV_HALO, :]
    y = acc + cb_ref[...]
    mu = jnp.mean(y, axis=-1, keepdims=True)
    yc = y - mu
    var = jnp.mean(yc * yc, axis=-1, keepdims=True)
    yn = yc * lax.rsqrt(var + LN_EPS) * lng_ref[...] + lnb_ref[...]
    yc_ref[...] = (yn * jax.nn.sigmoid(yn)).astype(BF16)


def _mix_in(x, g, w_in, conv_w, conv_b, ln_g, ln_b, bsz, seq):
    t = x.shape[0]
    nl = seq // ROW_TILE
    row = lambda w: pl.BlockSpec((ROW_TILE, w), lambda b, l: (b * nl + l, 0))
    return pl.pallas_call(
        _mix_in_body, grid=(bsz, nl),
        in_specs=[row(D_MODEL), _resident((1, D_MODEL)), _resident((D_MODEL, IN_WIDTH)),
                  _resident((CONV_TAPS, CONV_CH)), _resident((1, CONV_CH)), _resident((1, CONV_CH)),
                  _resident((1, CONV_CH))],
        out_specs=[row(CONV_CH), row(SSM_WIDTH)],
        out_shape=[jax.ShapeDtypeStruct((t, CONV_CH), BF16), jax.ShapeDtypeStruct((t, SSM_WIDTH), F32)],
        scratch_shapes=[pltpu.VMEM((ROW_TILE + CONV_HALO, CONV_CH), F32)],
        compiler_params=_params(2, 40 * 1024 * 1024), name="mix_in")(
            x, g.reshape(1, D_MODEL), w_in.astype(BF16), conv_w, conv_b.reshape(1, CONV_CH),
            ln_g.reshape(1, CONV_CH), ln_b.reshape(1, CONV_CH))


def _s5_tables(a_re, a_im, b_re, b_im, c_re, c_im, log_dt):
    dt = jnp.exp(log_dt.astype(F32))[:, None]
    a_re, a_im = a_re.astype(F32), a_im.astype(F32)
    steps = jnp.arange(S5_CHUNK + 1, dtype=F32)[:, None, None]
    mag = jnp.exp(steps * (dt * a_re)[None])
    ang = steps * (dt * a_im)[None]
    pw_re, pw_im = mag * jnp.cos(ang), mag * jnp.sin(ang)
    den = a_re * a_re + a_im * a_im
    nr, ni = pw_re[1] - 1.0, pw_im[1]
    q_re = (nr * a_re + ni * a_im) / den
    q_im = (ni * a_re - nr * a_im) / den
    bb_re = q_re[..., None] * b_re - q_im[..., None] * b_im
    bb_im = q_re[..., None] * b_im + q_im[..., None] * b_re
    lb_re = pw_re[:-1, ..., None] * bb_re[None] - pw_im[:-1, ..., None] * bb_im[None]
    lb_im = pw_re[:-1, ..., None] * bb_im[None] + pw_im[:-1, ..., None] * bb_re[None]
    hi = lax.Precision.HIGHEST
    k = (jnp.einsum("ghp,jgpi->jghi", c_re, lb_re, precision=hi)
         - jnp.einsum("ghp,jgpi->jghi", c_im, lb_im, precision=hi))
    r_idx = jnp.arange(S5_CHUNK)
    lag = r_idx[None, :] - r_idx[:, None]
    toep = jnp.where((lag >= 0)[..., None, None, None],
                     k[jnp.clip(lag, 0, S5_CHUNK - 1)], 0.0)
    toep = toep.transpose(2, 0, 4, 1, 3).reshape(SSM_GROUPS, S5_LANES, S5_LANES)
    to_state = lambda m: m[::-1].transpose(1, 0, 3, 2).reshape(SSM_GROUPS, S5_LANES, SSM_STATE)
    cl_re = c_re[None] * pw_re[1:, :, None, :] - c_im[None] * pw_im[1:, :, None, :]
    cl_im = c_re[None] * pw_im[1:, :, None, :] + c_im[None] * pw_re[1:, :, None, :]
    from_state = lambda m: m.transpose(1, 3, 0, 2).reshape(SSM_GROUPS, SSM_STATE, S5_LANES)
    a_chunk = jnp.stack([pw_re[S5_CHUNK], pw_im[S5_CHUNK]], axis=1)
    return toep, to_state(lb_re), to_state(lb_im), from_state(cl_re), from_state(-cl_im), a_chunk


def _s5_body(u_ref, t_ref, bre_ref, bim_ref, cre_ref, cim_ref, a_ref, y_ref, sre, sim, xre, xim, *, nb):
    hi = lax.Precision.HIGHEST
    u = u_ref[0]
    sre[...] = _dot(u, bre_ref[0], precision=hi)
    sim[...] = _dot(u, bim_ref[0], precision=hi)
    ar = a_ref[0, 0:1, :]
    ai = a_ref[0, 1:2, :]

    def step(c, carry):
        xr, xi = carry
        rows = pl.ds(pl.multiple_of(c * nb, nb), nb)
        xre[rows, :] = xr
        xim[rows, :] = xi
        return (ar * xr - ai * xi + sre[rows, :], ar * xi + ai * xr + sim[rows, :])

    zero = jnp.zeros((nb, SSM_STATE), F32)
    lax.fori_loop(0, u.shape[0] // nb, step, (zero, zero))
    y_ref[0] = (_dot(u, t_ref[0], precision=hi) + _dot(xre[...], cre_ref[0], precision=hi)
                + _dot(xim[...], cim_ref[0], precision=hi))


def _s5(u, tables, bsz, seq):
    nchunk = seq // S5_CHUNK
    rows = nchunk * bsz
    ug = u.reshape(bsz, nchunk, S5_CHUNK, SSM_GROUPS, SSM_GROUP).transpose(3, 1, 0, 2, 4)
    ug = ug.reshape(SSM_GROUPS, rows, S5_LANES)
    grp = lambda a, b: pl.BlockSpec((1, a, b), lambda g: (g, 0, 0))
    state = pltpu.VMEM((rows, SSM_STATE), F32)
    yg = pl.pallas_call(
        functools.partial(_s5_body, nb=bsz), grid=(SSM_GROUPS,),
        in_specs=[grp(rows, S5_LANES), grp(S5_LANES, S5_LANES), grp(S5_LANES, SSM_STATE),
                  grp(S5_LANES, SSM_STATE), grp(SSM_STATE, S5_LANES), grp(SSM_STATE, S5_LANES),
                  grp(2, SSM_STATE)],
        out_specs=grp(rows, S5_LANES),
        out_shape=jax.ShapeDtypeStruct((SSM_GROUPS, rows, S5_LANES), F32),
        scratch_shapes=[state, state, state, state],
        compiler_params=_params(1, 40 * 1024 * 1024), name="s5")(ug, *tables)
    yg = yg.reshape(SSM_GROUPS, nchunk, bsz, S5_CHUNK, SSM_GROUP).transpose(2, 1, 3, 0, 4)
    return yg.reshape(bsz * seq, SSM_WIDTH)


def _mix_out_body(x_ref, yc_ref, ys_ref, u_ref, d_ref, gw_ref, gb_ref, wc_ref, ws_ref, o_ref):
    y = ys_ref[...] + d_ref[...] * u_ref[...]
    y = 0.5 * y * (1.0 + jnp.tanh(math.sqrt(2.0 / math.pi) * (y + 0.044715 * (y * y * y))))
    gate = jax.nn.sigmoid(_dot(y.astype(BF16), gw_ref[...]) + gb_ref[...])
    y = (y * gate).astype(BF16)
    o_ref[...] = x_ref[...] + _dot(yc_ref[...], wc_ref[...]) + _dot(y, ws_ref[...])


def _mix_out(x, y_conv, y_state, u, d, glu_w, glu_b, w_out):
    t = x.shape[0]
    row = lambda w: pl.BlockSpec((ROW_TILE, w), lambda i: (i, 0))
    return pl.pallas_call(
        _mix_out_body, grid=(t // ROW_TILE,),
        in_specs=[row(D_MODEL), row(CONV_CH), row(SSM_WIDTH), row(SSM_WIDTH), _resident((1, SSM_WIDTH)),
                  _resident((SSM_WIDTH, SSM_WIDTH)), _resident((1, SSM_WIDTH)),
                  _resident((CONV_CH, D_MODEL)), _resident((SSM_WIDTH, D_MODEL))],
        out_specs=row(D_MODEL), out_shape=jax.ShapeDtypeStruct((t, D_MODEL), F32),
        compiler_params=_params(1, 40 * 1024 * 1024), name="mix_out")(
            x, y_conv, y_state, u, d.reshape(1, SSM_WIDTH), glu_w.astype(BF16),
            glu_b.reshape(1, SSM_WIDTH), w_out[:CONV_CH].astype(BF16), w_out[CONV_CH:].astype(BF16))


def _qkv_body(x_ref, g_ref, w_ref, qkv_ref, km_ref):
    h = _rms(x_ref[...], g_ref[...]).astype(BF16)
    r = _dot(h, w_ref[...])
    qkv_ref[...] = r.astype(BF16)
    for blk in range(x_ref.shape[0] // MOBA_BLOCK):
        k = r[blk * MOBA_BLOCK:(blk + 1) * MOBA_BLOCK, D_MODEL:2 * D_MODEL]
        km_ref[blk] = jnp.mean(k, axis=0, keepdims=True)


def _qkv(x, g, w_qkv):
    t = x.shape[0]
    nb = ROW_TILE // MOBA_BLOCK
    return pl.pallas_call(
        _qkv_body, grid=(t // ROW_TILE,),
        in_specs=[pl.BlockSpec((ROW_TILE, D_MODEL), lambda i: (i, 0)), _resident((1, D_MODEL)),
                  _resident((D_MODEL, 3 * D_MODEL))],
        out_specs=[pl.BlockSpec((ROW_TILE, 3 * D_MODEL), lambda i: (i, 0)),
                   pl.BlockSpec((nb, 1, D_MODEL), lambda i: (i, 0, 0))],
        out_shape=[jax.ShapeDtypeStruct((t, 3 * D_MODEL), BF16),
                   jax.ShapeDtypeStruct((t // MOBA_BLOCK, 1, D_MODEL), F32)],
        compiler_params=_params(1, 40 * 1024 * 1024), name="qkv")(
            x, g.reshape(1, D_MODEL), w_qkv.astype(BF16))


def _moba_body(q_ref, k_ref, v_ref, km_ref, o_ref, kaug_ref, vaug_ref):
    seq = q_ref.shape[0]
    nblk = seq // MOBA_BLOCK
    lane = lax.broadcasted_iota(jnp.int32, (seq, HEAD_DIM), 1)
    blk_of_row = lax.broadcasted_iota(jnp.int32, (seq, HEAD_DIM), 0) // MOBA_BLOCK
    kaug_ref[:, 0:HEAD_DIM] = k_ref[...]
    kaug_ref[:, HEAD_DIM:] = jnp.where(lane == blk_of_row, 1.0, 0.0).astype(BF16)
    vaug_ref[:, 0:HEAD_DIM] = v_ref[...]
    vaug_ref[:, HEAD_DIM:] = jnp.ones((seq, HEAD_DIM), BF16)
    km = km_ref[...].astype(BF16)
    blk_id = lax.broadcasted_iota(jnp.int32, (nblk, MOBA_BLOCK), 0)
    qi = lax.broadcasted_iota(jnp.int32, (MOBA_BLOCK, MOBA_BLOCK), 0)
    ki = lax.broadcasted_iota(jnp.int32, (MOBA_BLOCK, MOBA_BLOCK), 1)
    for i in range(nblk):
        own = slice(i * MOBA_BLOCK, (i + 1) * MOBA_BLOCK)
        q = q_ref[own, :]
        s_own = jnp.where(ki <= qi, _dot_nt(q, k_ref[own, :]), MASK_VALUE)
        m = jnp.max(s_own, axis=1, keepdims=True)
        if i > 0:
            past = blk_id < i
            if i > MOBA_TOPK:
                gate = _dot_nt(km, q)
                rank = jnp.zeros((nblk, MOBA_BLOCK), jnp.int32)
                for j in range(i):
                    gj = gate[j:j + 1, :]
                    ahead = jnp.where(gj > gate, 1, jnp.where(gj == gate, jnp.where(blk_id > j, 1, 0), 0))
                    rank = rank + ahead
                chosen = jnp.where(past, rank, MOBA_TOPK) < MOBA_TOPK
            else:
                chosen = past
            bias = jnp.where(chosen, 0.0, MASK_VALUE)
            bias = jnp.concatenate([bias, jnp.zeros((HEAD_DIM - nblk, MOBA_BLOCK), F32)], axis=0)
            q_aug = jnp.concatenate([q, bias.T.astype(BF16)], axis=1)
            s_past = _dot_nt(q_aug, kaug_ref[0:i * MOBA_BLOCK, :])
            m = jnp.maximum(m, jnp.max(s_past, axis=1, keepdims=True))
            p_past = jnp.exp2((s_past - m) * SOFTMAX_EXP2_SCALE).astype(BF16)
        p_own = jnp.exp2((s_own - m) * SOFTMAX_EXP2_SCALE).astype(BF16)
        o = _dot(p_own, vaug_ref[own, :])
        if i > 0:
            o = o + _dot(p_past, vaug_ref[0:i * MOBA_BLOCK, :])
        o_ref[own, :] = (o[:, :HEAD_DIM] / o[:, HEAD_DIM:]).astype(BF16)


def _moba(qkv, kmean, bsz, seq):
    nblk = seq // MOBA_BLOCK
    col = lambda off: pl.BlockSpec((seq, HEAD_DIM), lambda b, h: (b, off + h))
    return pl.pallas_call(
        _moba_body, grid=(bsz, N_HEADS),
        in_specs=[col(0), col(N_HEADS), col(2 * N_HEADS),
                  pl.BlockSpec((nblk, HEAD_DIM), lambda b, h: (b, h))],
        out_specs=col(0), out_shape=jax.ShapeDtypeStruct((bsz * seq, D_MODEL), BF16),
        scratch_shapes=[pltpu.VMEM((seq, 2 * HEAD_DIM), BF16), pltpu.VMEM((seq, 2 * HEAD_DIM), BF16)],
        compiler_params=_params(2, 40 * 1024 * 1024), name="moba")(qkv, qkv, qkv, kmean)


def _attn_out_body(x_ref, o_ref, w_ref, y_ref):
    y_ref[...] = x_ref[...] + _dot(o_ref[...], w_ref[...])


def _attn_out(x, o, w_o):
    t = x.shape[0]
    row = pl.BlockSpec((ROW_TILE, D_MODEL), lambda i: (i, 0))
    return pl.pallas_call(
        _attn_out_body, grid=(t // ROW_TILE,),
        in_specs=[row, row, _resident((D_MODEL, D_MODEL))],
        out_specs=row, out_shape=jax.ShapeDtypeStruct((t, D_MODEL), F32),
        compiler_params=_params(1, 32 * 1024 * 1024), name="attn_out")(x, o, w_o.astype(BF16))


def kernel(x, ffn_norm, ffn_w1, ffn_w3, ffn_w2, mix_norm, ab_w_in, conv_w, conv_b, conv_ln_g, conv_ln_b, ssm_a_re, ssm_a_im, ssm_b_re, ssm_b_im, ssm_c_re, ssm_c_im, ssm_d, ssm_log_dt, ssm_glu_w, ssm_glu_b, ab_w_out, attn_w_qkv, attn_w_o, final_norm):
    bsz, seq, d = x.shape
    depth = ffn_norm.shape[0]
    assert d == D_MODEL and seq % ROW_TILE == 0 and ROW_TILE % MOBA_BLOCK == 0
    assert bsz % V7X_SUBLANES == 0, "the S5 boundary scan walks whole sublane tiles of batch rows"
    x = x.reshape(bsz * seq, d)
    for l in range(depth):
        x = _ffn(x, ffn_norm[l, 0], ffn_w1[l, 0], ffn_w3[l, 0], ffn_w2[l, 0])
        if l % 2 == 0:
            e = l // 2
            y_conv, u = _mix_in(x, mix_norm[l], ab_w_in[e], conv_w[e], conv_b[e], conv_ln_g[e],
                                conv_ln_b[e], bsz, seq)
            tables = _s5_tables(ssm_a_re[e], ssm_a_im[e], ssm_b_re[e], ssm_b_im[e], ssm_c_re[e],
                                ssm_c_im[e], ssm_log_dt[e])
            y_state = _s5(u, tables, bsz, seq)
            x = _mix_out(x, y_conv, y_state, u, ssm_d[e], ssm_glu_w[e], ssm_glu_b[e], ab_w_out[e])
        else:
            o = l // 2
            qkv, kmean = _qkv(x, mix_norm[l], attn_w_qkv[o])
            att = _moba(qkv, kmean.reshape(-1, D_MODEL), bsz, seq)
            x = _attn_out(x, att, attn_w_o[o])
        last = l == depth - 1
        x = _ffn(x, ffn_norm[l, 1], ffn_w1[l, 1], ffn_w3[l, 1], ffn_w2[l, 1],
                 final_g=final_norm if last else None)
    return x.reshape(bsz, seq, d)
```

```python
import functools
import math

import jax
import jax.numpy as jnp
from jax import lax
from jax.experimental import pallas as pl
from jax.experimental.pallas import tpu as pltpu

F32 = jnp.float32
BF16 = jnp.bfloat16

D_MODEL = 1024
D_FF = 2816
RMS_EPS = 1e-6
LN_EPS = 1e-5
CONV_CH = 512
CONV_TAPS = 31
SSM_WIDTH = 512
SSM_GROUP = 16
SSM_GROUPS = SSM_WIDTH // SSM_GROUP
SSM_STATE = 64
IN_WIDTH = 2 * CONV_CH + SSM_WIDTH
N_HEADS = 8
HEAD_DIM = 128
MOBA_BLOCK = 256
MOBA_TOPK = 3

V7X_LANES = 128
V7X_SUBLANES = 8
V7X_VMEM_BYTES = 64 * 1024 * 1024

ROW_TILE = 512
FF_CHUNK = 1408
CONV_HALO = 32
CONV_ROWS = 64
S5_CHUNK = 16
S5_BLOCKS = SSM_WIDTH // V7X_LANES
S5_GPB = V7X_LANES // SSM_GROUP
S5_WIDE = S5_CHUNK * V7X_LANES
S5_COLS = 256
S5_ROWS_CHUNKS = 32
MASK_VALUE = -1e30
SOFTMAX_EXP2_SCALE = HEAD_DIM ** -0.5 * math.log2(math.e)


def _vmem_limit(n_bytes):
    return int(min(n_bytes, V7X_VMEM_BYTES - 8 * 1024 * 1024))


def _params(n_grid, vmem_bytes):
    return pltpu.CompilerParams(
        dimension_semantics=("arbitrary",) * n_grid, vmem_limit_bytes=_vmem_limit(vmem_bytes))


def _resident(shape):
    nd = len(shape)
    return pl.BlockSpec(shape, lambda *_: (0,) * nd, pipeline_mode=pl.Buffered(1))


def _rms(x, g):
    return x * lax.rsqrt(jnp.mean(x * x, axis=-1, keepdims=True) + RMS_EPS) * g


def _dot(a, b, **kw):
    return jnp.dot(a, b, preferred_element_type=F32, **kw)


def _dot_nt(a, b):
    return lax.dot_general(a, b, (((1,), (1,)), ((), ())), preferred_element_type=F32)


def _ffn_body(*refs, final):
    if final:
        x_ref, g_ref, w1_ref, w3_ref, w2_ref, gf_ref, o_ref = refs
    else:
        x_ref, g_ref, w1_ref, w3_ref, w2_ref, o_ref = refs
    x = x_ref[...]
    h = _rms(x, g_ref[...]).astype(BF16)
    acc = None
    for c0 in range(0, D_FF, FF_CHUNK):
        a = _dot(h, w1_ref[:, c0:c0 + FF_CHUNK])
        b = _dot(h, w3_ref[:, c0:c0 + FF_CHUNK])
        act = (a * jax.nn.sigmoid(a) * b).astype(BF16)
        part = _dot(act, w2_ref[c0:c0 + FF_CHUNK, :])
        acc = part if acc is None else acc + part
    y = x + 0.5 * acc
    if final:
        y = _rms(y, gf_ref[...])
    o_ref[...] = y


def _ffn(x, g, w1, w3, w2, final_g=None):
    t = x.shape[0]
    row = pl.BlockSpec((ROW_TILE, D_MODEL), lambda i: (i, 0))
    in_specs = [row, _resident((1, D_MODEL)), _resident((D_MODEL, D_FF)),
                _resident((D_MODEL, D_FF)), _resident((D_FF, D_MODEL))]
    args = [x, g.reshape(1, D_MODEL), w1.astype(BF16), w3.astype(BF16), w2.astype(BF16)]
    if final_g is not None:
        in_specs.append(_resident((1, D_MODEL)))
        args.append(final_g.reshape(1, D_MODEL))
    return pl.pallas_call(
        functools.partial(_ffn_body, final=final_g is not None),
        grid=(t // ROW_TILE,), in_specs=in_specs, out_specs=row,
        out_shape=jax.ShapeDtypeStruct((t, D_MODEL), F32),
        compiler_params=_params(1, 48 * 1024 * 1024), name="ffn")(*args)


def _mix_in_body(x_ref, g_ref, win_ref, cw_ref, cb_ref, lng_ref, lnb_ref, yc_ref, u_ref, u8_ref,
                 vpad_ref, vsh_ref, y_ref, ub_ref):
    tl = x_ref.shape[0]

    @pl.when(pl.program_id(1) == 0)
    def _():
        vpad_ref[0:CONV_HALO, :] = jnp.zeros((CONV_HALO, CONV_CH), F32)

    h = _rms(x_ref[...], g_ref[...]).astype(BF16)
    p = _dot(h, win_ref[...])
    u_ref[...] = p[:, 2 * CONV_CH:]
    vpad_ref[CONV_HALO:CONV_HALO + tl, :] = p[:, :CONV_CH] * jax.nn.sigmoid(p[:, CONV_CH:2 * CONV_CH])
    for gb in range(S5_BLOCKS):
        ub_ref[gb] = p[:, 2 * CONV_CH + gb * V7X_LANES:2 * CONV_CH + (gb + 1) * V7X_LANES]
        for s in range(S5_CHUNK):
            u8_ref[gb, :, s * V7X_LANES:(s + 1) * V7X_LANES] = ub_ref[
                gb, pl.ds(s, tl // S5_CHUNK, stride=S5_CHUNK), :].astype(BF16)
    first = CONV_HALO - (CONV_TAPS - 1)
    for s in range(V7X_SUBLANES):
        span = tl + (CONV_TAPS - 1 - s) // V7X_SUBLANES * V7X_SUBLANES
        vsh_ref[s, 0:span, :] = vpad_ref[first + s:first + s + span, :]
    vpad_ref[0:CONV_HALO, :] = vpad_ref[tl:tl + CONV_HALO, :]
    for lb in range(CONV_CH // V7X_LANES):
        lanes = slice(lb * V7X_LANES, (lb + 1) * V7X_LANES)
        w = cw_ref[:, lanes]
        bias = cb_ref[:, lanes]

        def rows(rb, carry):
            r0 = pl.multiple_of(rb * CONV_ROWS, CONV_ROWS)
            acc = jnp.zeros((CONV_ROWS, V7X_LANES), F32)
            for j in range(CONV_TAPS):
                q, s = divmod(j, V7X_SUBLANES)
                acc = acc + w[j:j + 1, :] * vsh_ref[s, pl.ds(r0 + q * V7X_SUBLANES, CONV_ROWS), lanes]
            y_ref[pl.ds(r0, CONV_ROWS), lanes] = acc + bias
            return carry

        lax.fori_loop(0, tl // CONV_ROWS, rows, 0)
    y = y_ref[...]
    mu = jnp.mean(y, axis=-1, keepdims=True)
    yc = y - mu
    var = jnp.mean(yc * yc, axis=-1, keepdims=True)
    yn = yc * lax.rsqrt(var + LN_EPS) * lng_ref[...] + lnb_ref[...]
    yc_ref[...] = (yn * jax.nn.sigmoid(yn)).astype(BF16)


def _mix_in(x, g, w_in, conv_w, conv_b, ln_g, ln_b, bsz, seq):
    t = x.shape[0]
    nl = seq // ROW_TILE
    row = lambda w: pl.BlockSpec((ROW_TILE, w), lambda b, l: (b * nl + l, 0))
    return pl.pallas_call(
        _mix_in_body, grid=(bsz, nl),
        in_specs=[row(D_MODEL), _resident((1, D_MODEL)), _resident((D_MODEL, IN_WIDTH)),
                  _resident((CONV_TAPS, CONV_CH)), _resident((1, CONV_CH)), _resident((1, CONV_CH)),
                  _resident((1, CONV_CH))],
        out_specs=[row(CONV_CH), row(SSM_WIDTH), _chunk_layout_spec()],
        out_shape=[jax.ShapeDtypeStruct((t, CONV_CH), BF16), jax.ShapeDtypeStruct((t, SSM_WIDTH), F32),
                   jax.ShapeDtypeStruct((S5_BLOCKS, seq // S5_CHUNK, bsz * S5_WIDE), BF16)],
        scratch_shapes=[pltpu.VMEM((ROW_TILE + CONV_HALO, CONV_CH), F32),
                        pltpu.VMEM((V7X_SUBLANES, ROW_TILE + CONV_HALO - V7X_SUBLANES, CONV_CH), F32),
                        pltpu.VMEM((ROW_TILE, CONV_CH), F32),
                        pltpu.VMEM((S5_BLOCKS, ROW_TILE, V7X_LANES), F32)],
        compiler_params=_params(2, 48 * 1024 * 1024), name="mix_in")(
            x, g.reshape(1, D_MODEL), w_in.astype(BF16), conv_w, conv_b.reshape(1, CONV_CH),
            ln_g.reshape(1, CONV_CH), ln_b.reshape(1, CONV_CH))


def _s5_tables(a_re, a_im, b_re, b_im, c_re, c_im, log_dt):
    dt = jnp.exp(log_dt.astype(F32))[:, None]
    a_re, a_im = a_re.astype(F32), a_im.astype(F32)
    steps = jnp.arange(S5_CHUNK + 1, dtype=F32)[:, None, None]
    mag = jnp.exp(steps * (dt * a_re)[None])
    ang = steps * (dt * a_im)[None]
    pw_re, pw_im = mag * jnp.cos(ang), mag * jnp.sin(ang)
    den = a_re * a_re + a_im * a_im
    nr, ni = pw_re[1] - 1.0, pw_im[1]
    q_re = (nr * a_re + ni * a_im) / den
    q_im = (ni * a_re - nr * a_im) / den
    bb_re = q_re[..., None] * b_re - q_im[..., None] * b_im
    bb_im = q_re[..., None] * b_im + q_im[..., None] * b_re
    lb_re = pw_re[:-1, ..., None] * bb_re[None] - pw_im[:-1, ..., None] * bb_im[None]
    lb_im = pw_re[:-1, ..., None] * bb_im[None] + pw_im[:-1, ..., None] * bb_re[None]
    hi = lax.Precision.HIGHEST
    k = (jnp.einsum("ghp,jgpi->jghi", c_re, lb_re, precision=hi)
         - jnp.einsum("ghp,jgpi->jghi", c_im, lb_im, precision=hi))
    eye = jnp.eye(S5_GPB, dtype=F32)
    blocked = lambda m: m.reshape(m.shape[:-3] + (S5_BLOCKS, S5_GPB) + m.shape[-2:])
    onehot_lag = (jnp.arange(S5_CHUNK)[None, :, None] - jnp.arange(S5_CHUNK)[:, None, None]
                  == jnp.arange(S5_CHUNK)[None, None, :]).astype(F32)
    toep = jnp.einsum("rsj,jghi->rsghi", onehot_lag, k, precision=hi)
    toep = blocked(toep).transpose(2, 0, 3, 5, 1, 4)
    toep = toep[:, :, :, :, :, None, :] * eye[None, None, :, None, None, :, None]
    toep = toep.reshape(S5_BLOCKS, S5_WIDE, S5_WIDE).astype(BF16)

    def to_state(m):
        m = blocked(m[::-1]).transpose(1, 0, 2, 4, 3)
        m = m[:, :, :, :, None, :] * eye[None, None, :, None, :, None]
        return m.reshape(S5_BLOCKS, S5_WIDE, S5_GPB * SSM_STATE)

    def from_state(m):
        m = blocked(m).transpose(1, 2, 4, 0, 3)
        m = m[:, :, :, :, None, :] * eye[None, :, None, None, :, None]
        return m.reshape(S5_BLOCKS, S5_GPB * SSM_STATE, S5_WIDE)

    cl_re = c_re[None] * pw_re[1:, :, None, :] - c_im[None] * pw_im[1:, :, None, :]
    cl_im = c_re[None] * pw_im[1:, :, None, :] + c_im[None] * pw_re[1:, :, None, :]
    in_tab = jnp.concatenate([to_state(lb_re), to_state(lb_im)], axis=2).astype(BF16)
    out_tab = jnp.concatenate([from_state(cl_re), from_state(-cl_im)], axis=1).astype(BF16)
    a_chunk = jnp.stack([pw_re[S5_CHUNK].reshape(S5_BLOCKS, -1), pw_im[S5_CHUNK].reshape(S5_BLOCKS, -1)], axis=1)
    return toep, in_tab, out_tab, a_chunk


def _s5_body(u_ref, t_ref, b_ref, c_ref, a_ref, y_ref, s_ref, x_ref, xr_ref, xi_ref, *, nb):
    half = S5_GPB * SSM_STATE

    @pl.when(pl.program_id(1) == 0)
    def _():
        xr_ref[...] = jnp.zeros((nb, half), F32)
        xi_ref[...] = jnp.zeros((nb, half), F32)

    s_ref[...] = _dot(u_ref[...], b_ref[...])
    ar = a_ref[0:1, :]
    ai = a_ref[1:2, :]

    def step(c, carry):
        xr, xi = carry
        rows = pl.ds(pl.multiple_of(c * nb, nb), nb)
        x_ref[rows, 0:half] = xr.astype(BF16)
        x_ref[rows, half:] = xi.astype(BF16)
        return (ar * xr - ai * xi + s_ref[rows, 0:half], ar * xi + ai * xr + s_ref[rows, half:])

    xr, xi = lax.fori_loop(0, u_ref.shape[0] // nb, step, (xr_ref[...], xi_ref[...]))
    xr_ref[...] = xr
    xi_ref[...] = xi
    for nt in range(S5_WIDE // S5_COLS):
        cols = slice(nt * S5_COLS, (nt + 1) * S5_COLS)
        k = (nt + 1) * S5_COLS
        y_ref[:, cols] = _dot(u_ref[:, 0:k], t_ref[0:k, cols]) + _dot(x_ref[...], c_ref[:, cols])


def _s5(u8, tables, bsz, seq):
    nchunk = seq // S5_CHUNK
    rows = S5_ROWS_CHUNKS * bsz
    half = S5_GPB * SSM_STATE
    tab = lambda a, b: pl.BlockSpec((None, a, b), lambda g, c: (g, 0, 0), pipeline_mode=pl.Buffered(1))
    act = pl.BlockSpec((None, rows, S5_WIDE), lambda g, c: (g, c, 0))
    return pl.pallas_call(
        functools.partial(_s5_body, nb=bsz), grid=(S5_BLOCKS, nchunk // S5_ROWS_CHUNKS),
        in_specs=[act, tab(S5_WIDE, S5_WIDE), tab(S5_WIDE, 2 * half), tab(2 * half, S5_WIDE), tab(2, half)],
        out_specs=act, out_shape=jax.ShapeDtypeStruct((S5_BLOCKS, nchunk * bsz, S5_WIDE), F32),
        scratch_shapes=[pltpu.VMEM((rows, 2 * half), F32), pltpu.VMEM((rows, 2 * half), BF16),
                        pltpu.VMEM((bsz, half), F32), pltpu.VMEM((bsz, half), F32)],
        compiler_params=_params(2, 48 * 1024 * 1024), name="s5")(u8, *tables)


def _chunk_layout_spec():
    return pl.BlockSpec((S5_BLOCKS, ROW_TILE // S5_CHUNK, S5_WIDE), lambda b, l: (0, l, b))


def _mix_out_body(x_ref, yc_ref, y8_ref, u_ref, d_ref, gw_ref, gb_ref, wc_ref, ws_ref, o_ref, ys_ref):
    tl = x_ref.shape[0]
    for gb in range(S5_BLOCKS):
        for s in range(S5_CHUNK):
            ys_ref[gb, pl.ds(s, tl // S5_CHUNK, stride=S5_CHUNK), :] = (
                y8_ref[gb, :, s * V7X_LANES:(s + 1) * V7X_LANES])
    ys = jnp.concatenate([ys_ref[gb] for gb in range(S5_BLOCKS)], axis=1)
    y = ys + d_ref[...] * u_ref[...]
    y = 0.5 * y * (1.0 + jnp.tanh(math.sqrt(2.0 / math.pi) * (y + 0.044715 * (y * y * y))))
    gate = jax.nn.sigmoid(_dot(y.astype(BF16), gw_ref[...]) + gb_ref[...])
    y = (y * gate).astype(BF16)
    o_ref[...] = x_ref[...] + _dot(yc_ref[...], wc_ref[...]) + _dot(y, ws_ref[...])


def _mix_out(x, y_conv, y_state, u, d, glu_w, glu_b, w_out, bsz, seq):
    t = x.shape[0]
    nl = seq // ROW_TILE
    row = lambda w: pl.BlockSpec((ROW_TILE, w), lambda b, l: (b * nl + l, 0))
    return pl.pallas_call(
        _mix_out_body, grid=(bsz, nl),
        in_specs=[row(D_MODEL), row(CONV_CH), _chunk_layout_spec(), row(SSM_WIDTH), _resident((1, SSM_WIDTH)),
                  _resident((SSM_WIDTH, SSM_WIDTH)), _resident((1, SSM_WIDTH)),
                  _resident((CONV_CH, D_MODEL)), _resident((SSM_WIDTH, D_MODEL))],
        out_specs=row(D_MODEL), out_shape=jax.ShapeDtypeStruct((t, D_MODEL), F32),
        scratch_shapes=[pltpu.VMEM((S5_BLOCKS, ROW_TILE, V7X_LANES), F32)],
        compiler_params=_params(2, 40 * 1024 * 1024), name="mix_out")(
            x, y_conv, y_state, u, d.reshape(1, SSM_WIDTH), glu_w.astype(BF16),
            glu_b.reshape(1, SSM_WIDTH), w_out[:CONV_CH].astype(BF16), w_out[CONV_CH:].astype(BF16))


def _qkv_body(x_ref, g_ref, w_ref, qkv_ref, km_ref):
    h = _rms(x_ref[...], g_ref[...]).astype(BF16)
    r = _dot(h, w_ref[...])
    qkv_ref[...] = r.astype(BF16)
    for blk in range(x_ref.shape[0] // MOBA_BLOCK):
        k = r[blk * MOBA_BLOCK:(blk + 1) * MOBA_BLOCK, D_MODEL:2 * D_MODEL]
        km_ref[blk] = jnp.mean(k, axis=0, keepdims=True)


def _qkv(x, g, w_qkv):
    t = x.shape[0]
    nb = ROW_TILE // MOBA_BLOCK
    return pl.pallas_call(
        _qkv_body, grid=(t // ROW_TILE,),
        in_specs=[pl.BlockSpec((ROW_TILE, D_MODEL), lambda i: (i, 0)), _resident((1, D_MODEL)),
                  _resident((D_MODEL, 3 * D_MODEL))],
        out_specs=[pl.BlockSpec((ROW_TILE, 3 * D_MODEL), lambda i: (i, 0)),
                   pl.BlockSpec((nb, 1, D_MODEL), lambda i: (i, 0, 0))],
        out_shape=[jax.ShapeDtypeStruct((t, 3 * D_MODEL), BF16),
                   jax.ShapeDtypeStruct((t // MOBA_BLOCK, 1, D_MODEL), F32)],
        compiler_params=_params(1, 40 * 1024 * 1024), name="qkv")(
            x, g.reshape(1, D_MODEL), w_qkv.astype(BF16))


def _moba_body(q_ref, k_ref, v_ref, km_ref, o_ref, kaug_ref, vaug_ref):
    seq = q_ref.shape[0]
    nblk = seq // MOBA_BLOCK
    lane = lax.broadcasted_iota(jnp.int32, (seq, HEAD_DIM), 1)
    blk_of_row = lax.broadcasted_iota(jnp.int32, (seq, HEAD_DIM), 0) // MOBA_BLOCK
    kaug_ref[:, 0:HEAD_DIM] = k_ref[...]
    kaug_ref[:, HEAD_DIM:] = jnp.where(lane == blk_of_row, 1.0, 0.0).astype(BF16)
    vaug_ref[:, 0:HEAD_DIM] = v_ref[...]
    vaug_ref[:, HEAD_DIM:] = jnp.ones((seq, HEAD_DIM), BF16)
    km = km_ref[...].astype(BF16)
    blk_id = lax.broadcasted_iota(jnp.int32, (nblk, MOBA_BLOCK), 0)
    qi = lax.broadcasted_iota(jnp.int32, (MOBA_BLOCK, MOBA_BLOCK), 0)
    ki = lax.broadcasted_iota(jnp.int32, (MOBA_BLOCK, MOBA_BLOCK), 1)
    for i in range(nblk):
        own = slice(i * MOBA_BLOCK, (i + 1) * MOBA_BLOCK)
        q = q_ref[own, :]
        s_own = jnp.where(ki <= qi, _dot_nt(q, k_ref[own, :]), MASK_VALUE)
        m = jnp.max(s_own, axis=1, keepdims=True)
        if i > 0:
            past = blk_id < i
            if i > MOBA_TOPK:
                gate = _dot_nt(km, q)
                rank = jnp.zeros((nblk, MOBA_BLOCK), jnp.int32)
                for j in range(i):
                    gj = gate[j:j + 1, :]
                    ahead = jnp.where(gj > gate, 1, jnp.where(gj == gate, jnp.where(blk_id > j, 1, 0), 0))
                    rank = rank + ahead
                chosen = jnp.where(past, rank, MOBA_TOPK) < MOBA_TOPK
            else:
                chosen = past
            bias = jnp.where(chosen, 0.0, MASK_VALUE)
            bias = jnp.concatenate([bias, jnp.zeros((HEAD_DIM - nblk, MOBA_BLOCK), F32)], axis=0)
            q_aug = jnp.concatenate([q, bias.T.astype(BF16)], axis=1)
            s_past = _dot_nt(q_aug, kaug_ref[0:i * MOBA_BLOCK, :])
            m = jnp.maximum(m, jnp.max(s_past, axis=1, keepdims=True))
            p_past = jnp.exp2((s_past - m) * SOFTMAX_EXP2_SCALE).astype(BF16)
        p_own = jnp.exp2((s_own - m) * SOFTMAX_EXP2_SCALE).astype(BF16)
        o = _dot(p_own, vaug_ref[own, :])
        if i > 0:
            o = o + _dot(p_past, vaug_ref[0:i * MOBA_BLOCK, :])
        o_ref[own, :] = (o[:, :HEAD_DIM] / o[:, HEAD_DIM:]).astype(BF16)


def _moba(qkv, kmean, bsz, seq):
    nblk = seq // MOBA_BLOCK
    col = lambda off: pl.BlockSpec((seq, HEAD_DIM), lambda b, h: (b, off + h))
    return pl.pallas_call(
        _moba_body, grid=(bsz, N_HEADS),
        in_specs=[col(0), col(N_HEADS), col(2 * N_HEADS),
                  pl.BlockSpec((nblk, HEAD_DIM), lambda b, h: (b, h))],
        out_specs=col(0), out_shape=jax.ShapeDtypeStruct((bsz * seq, D_MODEL), BF16),
        scratch_shapes=[pltpu.VMEM((seq, 2 * HEAD_DIM), BF16), pltpu.VMEM((seq, 2 * HEAD_DIM), BF16)],
        compiler_params=_params(2, 40 * 1024 * 1024), name="moba")(qkv, qkv, qkv, kmean)


def _attn_out_body(x_ref, o_ref, w_ref, y_ref):
    y_ref[...] = x_ref[...] + _dot(o_ref[...], w_ref[...])


def _attn_out(x, o, w_o):
    t = x.shape[0]
    row = pl.BlockSpec((ROW_TILE, D_MODEL), lambda i: (i, 0))
    return pl.pallas_call(
        _attn_out_body, grid=(t // ROW_TILE,),
        in_specs=[row, row, _resident((D_MODEL, D_MODEL))],
        out_specs=row, out_shape=jax.ShapeDtypeStruct((t, D_MODEL), F32),
        compiler_params=_params(1, 32 * 1024 * 1024), name="attn_out")(x, o, w_o.astype(BF16))


def kernel(x, ffn_norm, ffn_w1, ffn_w3, ffn_w2, mix_norm, ab_w_in, conv_w, conv_b, conv_ln_g, conv_ln_b, ssm_a_re, ssm_a_im, ssm_b_re, ssm_b_im, ssm_c_re, ssm_c_im, ssm_d, ssm_log_dt, ssm_glu_w, ssm_glu_b, ab_w_out, attn_w_qkv, attn_w_o, final_norm):
    bsz, seq, d = x.shape
    depth = ffn_norm.shape[0]
    assert d == D_MODEL and seq % ROW_TILE == 0 and ROW_TILE % MOBA_BLOCK == 0
    assert bsz % (2 * V7X_SUBLANES) == 0, "the S5 boundary scan walks whole packed bf16 tiles of batch rows"
    x = x.reshape(bsz * seq, d)
    for l in range(depth):
        x = _ffn(x, ffn_norm[l, 0], ffn_w1[l, 0], ffn_w3[l, 0], ffn_w2[l, 0])
        if l % 2 == 0:
            e = l // 2
            y_conv, u, u8 = _mix_in(x, mix_norm[l], ab_w_in[e], conv_w[e], conv_b[e], conv_ln_g[e],
                                    conv_ln_b[e], bsz, seq)
            tables = _s5_tables(ssm_a_re[e], ssm_a_im[e], ssm_b_re[e], ssm_b_im[e], ssm_c_re[e],
                                ssm_c_im[e], ssm_log_dt[e])
            nchunk = seq // S5_CHUNK
            y8 = _s5(u8.reshape(S5_BLOCKS, nchunk * bsz, S5_WIDE), tables, bsz, seq)
            x = _mix_out(x, y_conv, y8.reshape(S5_BLOCKS, nchunk, bsz * S5_WIDE), u, ssm_d[e], ssm_glu_w[e],
                         ssm_glu_b[e], ab_w_out[e], bsz, seq)
        else:
            o = l // 2
            qkv, kmean = _qkv(x, mix_norm[l], attn_w_qkv[o])
            att = _moba(qkv, kmean.reshape(-1, D_MODEL), bsz, seq)
            x = _attn_out(x, att, attn_w_o[o])
        last = l == depth - 1
        x = _ffn(x, ffn_norm[l, 1], ffn_w1[l, 1], ffn_w3[l, 1], ffn_w2[l, 1],
                 final_g=final_norm if last else None)
    return x.reshape(bsz, seq, d)
```

```python
import functools
import math

import jax
import jax.numpy as jnp
from jax import lax
from jax.experimental import pallas as pl
from jax.experimental.pallas import tpu as pltpu

F32 = jnp.float32
BF16 = jnp.bfloat16

D_MODEL = 1024
D_FF = 2816
RMS_EPS = 1e-6
LN_EPS = 1e-5
CONV_CH = 512
CONV_TAPS = 31
SSM_WIDTH = 512
SSM_GROUP = 16
SSM_GROUPS = SSM_WIDTH // SSM_GROUP
SSM_STATE = 64
IN_WIDTH = 2 * CONV_CH + SSM_WIDTH
N_HEADS = 8
HEAD_DIM = 128
MOBA_BLOCK = 256
MOBA_TOPK = 3

V7X_LANES = 128
V7X_SUBLANES = 8
V7X_VMEM_BYTES = 64 * 1024 * 1024

ROW_TILE = 512
FF_CHUNK = 1408
CONV_HALO = 32
CONV_ROWS = 128
S5_CHUNK = 16
S5_BLOCKS = SSM_WIDTH // V7X_LANES
S5_GPB = V7X_LANES // SSM_GROUP
S5_WIDE = S5_CHUNK * V7X_LANES
S5_COLS = 256
S5_ROWS_CHUNKS = 32
MASK_VALUE = -1e30
SOFTMAX_EXP2_SCALE = HEAD_DIM ** -0.5 * math.log2(math.e)


def _vmem_limit(n_bytes):
    return int(min(n_bytes, V7X_VMEM_BYTES - 8 * 1024 * 1024))


def _params(n_grid, vmem_bytes):
    return pltpu.CompilerParams(
        dimension_semantics=("arbitrary",) * n_grid, vmem_limit_bytes=_vmem_limit(vmem_bytes))


def _resident(shape):
    nd = len(shape)
    return pl.BlockSpec(shape, lambda *_: (0,) * nd, pipeline_mode=pl.Buffered(1))


def _rms(x, g):
    return x * lax.rsqrt(jnp.mean(x * x, axis=-1, keepdims=True) + RMS_EPS) * g


def _dot(a, b, **kw):
    return jnp.dot(a, b, preferred_element_type=F32, **kw)


def _dot_nt(a, b):
    return lax.dot_general(a, b, (((1,), (1,)), ((), ())), preferred_element_type=F32)


def _ffn_body(*refs, final):
    if final:
        x_ref, g_ref, w1_ref, w3_ref, w2_ref, gf_ref, o_ref = refs
    else:
        x_ref, g_ref, w1_ref, w3_ref, w2_ref, o_ref = refs
    x = x_ref[...]
    h = _rms(x, g_ref[...]).astype(BF16)
    acc = None
    for c0 in range(0, D_FF, FF_CHUNK):
        a = _dot(h, w1_ref[:, c0:c0 + FF_CHUNK])
        b = _dot(h, w3_ref[:, c0:c0 + FF_CHUNK])
        act = (a * jax.nn.sigmoid(a) * b).astype(BF16)
        part = _dot(act, w2_ref[c0:c0 + FF_CHUNK, :])
        acc = part if acc is None else acc + part
    y = x + 0.5 * acc
    if final:
        y = _rms(y, gf_ref[...])
    o_ref[...] = y


def _ffn(x, g, w1, w3, w2, final_g=None):
    t = x.shape[0]
    row = pl.BlockSpec((ROW_TILE, D_MODEL), lambda i: (i, 0))
    in_specs = [row, _resident((1, D_MODEL)), _resident((D_MODEL, D_FF)),
                _resident((D_MODEL, D_FF)), _resident((D_FF, D_MODEL))]
    args = [x, g.reshape(1, D_MODEL), w1.astype(BF16), w3.astype(BF16), w2.astype(BF16)]
    if final_g is not None:
        in_specs.append(_resident((1, D_MODEL)))
        args.append(final_g.reshape(1, D_MODEL))
    return pl.pallas_call(
        functools.partial(_ffn_body, final=final_g is not None),
        grid=(t // ROW_TILE,), in_specs=in_specs, out_specs=row,
        out_shape=jax.ShapeDtypeStruct((t, D_MODEL), F32),
        compiler_params=_params(1, 48 * 1024 * 1024), name="ffn")(*args)


def _mix_in_body(x_ref, g_ref, win_ref, cw_ref, cb_ref, lng_ref, lnb_ref, yc_ref, u_ref, u8_ref,
                 vpad_ref, vsh_ref, y_ref, ub_ref):
    tl = x_ref.shape[0]

    @pl.when(pl.program_id(1) == 0)
    def _():
        vpad_ref[0:CONV_HALO, :] = jnp.zeros((CONV_HALO, CONV_CH), F32)

    h = _rms(x_ref[...], g_ref[...]).astype(BF16)
    p = _dot(h, win_ref[...])
    u_ref[...] = p[:, 2 * CONV_CH:]
    vpad_ref[CONV_HALO:CONV_HALO + tl, :] = p[:, :CONV_CH] * jax.nn.sigmoid(p[:, CONV_CH:2 * CONV_CH])
    for gb in range(S5_BLOCKS):
        ub_ref[gb] = p[:, 2 * CONV_CH + gb * V7X_LANES:2 * CONV_CH + (gb + 1) * V7X_LANES]
        for s in range(S5_CHUNK):
            u8_ref[gb, :, s * V7X_LANES:(s + 1) * V7X_LANES] = ub_ref[
                gb, pl.ds(s, tl // S5_CHUNK, stride=S5_CHUNK), :].astype(BF16)
    first = CONV_HALO - (CONV_TAPS - 1)
    for s in range(V7X_SUBLANES):
        span = tl + (CONV_TAPS - 1 - s) // V7X_SUBLANES * V7X_SUBLANES
        vsh_ref[s, 0:span, :] = vpad_ref[first + s:first + s + span, :]
    vpad_ref[0:CONV_HALO, :] = vpad_ref[tl:tl + CONV_HALO, :]
    for lb in range(CONV_CH // V7X_LANES):
        lanes = slice(lb * V7X_LANES, (lb + 1) * V7X_LANES)
        def rows(rb, carry):
            r0 = pl.multiple_of(rb * CONV_ROWS, CONV_ROWS)
            acc = jnp.zeros((CONV_ROWS, V7X_LANES), F32)
            for j in range(CONV_TAPS):
                q, s = divmod(j, V7X_SUBLANES)
                acc = acc + cw_ref[j:j + 1, lanes] * vsh_ref[s, pl.ds(r0 + q * V7X_SUBLANES, CONV_ROWS), lanes]
            y_ref[pl.ds(r0, CONV_ROWS), lanes] = acc + cb_ref[:, lanes]
            return carry

        lax.fori_loop(0, tl // CONV_ROWS, rows, 0)
    y = y_ref[...]
    mu = jnp.mean(y, axis=-1, keepdims=True)
    yc = y - mu
    var = jnp.mean(yc * yc, axis=-1, keepdims=True)
    yn = yc * lax.rsqrt(var + LN_EPS) * lng_ref[...] + lnb_ref[...]
    yc_ref[...] = (yn * jax.nn.sigmoid(yn)).astype(BF16)


def _mix_in(x, g, w_in, conv_w, conv_b, ln_g, ln_b, bsz, seq):
    t = x.shape[0]
    nl = seq // ROW_TILE
    row = lambda w: pl.BlockSpec((ROW_TILE, w), lambda b, l: (b * nl + l, 0))
    return pl.pallas_call(
        _mix_in_body, grid=(bsz, nl),
        in_specs=[row(D_MODEL), _resident((1, D_MODEL)), _resident((D_MODEL, IN_WIDTH)),
                  _resident((CONV_TAPS, CONV_CH)), _resident((1, CONV_CH)), _resident((1, CONV_CH)),
                  _resident((1, CONV_CH))],
        out_specs=[row(CONV_CH), row(SSM_WIDTH), _chunk_layout_spec()],
        out_shape=[jax.ShapeDtypeStruct((t, CONV_CH), BF16), jax.ShapeDtypeStruct((t, SSM_WIDTH), F32),
                   jax.ShapeDtypeStruct((S5_BLOCKS, seq // S5_CHUNK, bsz * S5_WIDE), BF16)],
        scratch_shapes=[pltpu.VMEM((ROW_TILE + CONV_HALO, CONV_CH), F32),
                        pltpu.VMEM((V7X_SUBLANES, ROW_TILE + CONV_HALO - V7X_SUBLANES, CONV_CH), F32),
                        pltpu.VMEM((ROW_TILE, CONV_CH), F32),
                        pltpu.VMEM((S5_BLOCKS, ROW_TILE, V7X_LANES), F32)],
        compiler_params=_params(2, 48 * 1024 * 1024), name="mix_in")(
            x, g.reshape(1, D_MODEL), w_in.astype(BF16), conv_w, conv_b.reshape(1, CONV_CH),
            ln_g.reshape(1, CONV_CH), ln_b.reshape(1, CONV_CH))


def _s5_tables(a_re, a_im, b_re, b_im, c_re, c_im, log_dt):
    dt = jnp.exp(log_dt.astype(F32))[:, None]
    a_re, a_im = a_re.astype(F32), a_im.astype(F32)
    steps = jnp.arange(S5_CHUNK + 1, dtype=F32)[:, None, None]
    mag = jnp.exp(steps * (dt * a_re)[None])
    ang = steps * (dt * a_im)[None]
    pw_re, pw_im = mag * jnp.cos(ang), mag * jnp.sin(ang)
    den = a_re * a_re + a_im * a_im
    nr, ni = pw_re[1] - 1.0, pw_im[1]
    q_re = (nr * a_re + ni * a_im) / den
    q_im = (ni * a_re - nr * a_im) / den
    bb_re = q_re[..., None] * b_re - q_im[..., None] * b_im
    bb_im = q_re[..., None] * b_im + q_im[..., None] * b_re
    lb_re = pw_re[:-1, ..., None] * bb_re[None] - pw_im[:-1, ..., None] * bb_im[None]
    lb_im = pw_re[:-1, ..., None] * bb_im[None] + pw_im[:-1, ..., None] * bb_re[None]
    hi = lax.Precision.HIGHEST
    k = (jnp.einsum("ghp,jgpi->jghi", c_re, lb_re, precision=hi)
         - jnp.einsum("ghp,jgpi->jghi", c_im, lb_im, precision=hi))
    cl_re = c_re[None] * pw_re[1:, :, None, :] - c_im[None] * pw_im[1:, :, None, :]
    cl_im = c_re[None] * pw_im[1:, :, None, :] + c_im[None] * pw_re[1:, :, None, :]

    def per_block(m):
        m = m.transpose(1, 0, 3, 2).reshape(S5_BLOCKS, S5_GPB, S5_CHUNK, m.shape[3], m.shape[2])
        return m.transpose(0, 2, 1, 3, 4).reshape(S5_BLOCKS, S5_CHUNK, S5_GPB * m.shape[3], m.shape[4])

    kc = per_block(k).astype(BF16)
    lbc = jnp.stack([per_block(lb_re[::-1]), per_block(lb_im[::-1])], axis=2).astype(BF16)
    clc = jnp.stack([per_block(cl_re), per_block(-cl_im)], axis=2).astype(BF16)
    a_chunk = jnp.stack([pw_re[S5_CHUNK].reshape(S5_BLOCKS, -1), pw_im[S5_CHUNK].reshape(S5_BLOCKS, -1)], axis=1)
    return _s5_expand(kc, lbc, clc) + (a_chunk,)


def _s5_expand_body(kc_ref, lbc_ref, clc_ref, toep_ref, in_ref, out_ref):
    r = pl.program_id(1)
    half = S5_GPB * SSM_STATE
    iota = lambda shape, d: lax.broadcasted_iota(jnp.int32, shape, d)
    one_hot = lambda c: jnp.where(c, 1.0, 0.0).astype(BF16)
    shift_g, shift_p = SSM_GROUP.bit_length() - 1, SSM_STATE.bit_length() - 1
    rep_h = one_hot((iota((SSM_GROUP, V7X_LANES), 1) & (SSM_GROUP - 1)) == iota((SSM_GROUP, V7X_LANES), 0))
    rep_p = one_hot((iota((SSM_STATE, half), 1) & (SSM_STATE - 1)) == iota((SSM_STATE, half), 0))
    same_hh = (iota((V7X_LANES, V7X_LANES), 0) >> shift_g) == (iota((V7X_LANES, V7X_LANES), 1) >> shift_g)
    same_hp = (iota((V7X_LANES, half), 0) >> shift_g) == (iota((V7X_LANES, half), 1) >> shift_p)
    same_ph = (iota((half, V7X_LANES), 0) >> shift_p) == (iota((half, V7X_LANES), 1) >> shift_g)
    for s in range(S5_CHUNK):
        blk = _dot(kc_ref[jnp.maximum(s - r, 0)], rep_h)
        blk = jnp.where(same_hh, blk, 0.0) * jnp.where(s >= r, 1.0, 0.0)
        toep_ref[:, s * V7X_LANES:(s + 1) * V7X_LANES] = blk.astype(BF16)
    for ri in range(2):
        in_ref[:, ri * half:(ri + 1) * half] = jnp.where(same_hp, _dot(lbc_ref[ri], rep_p), 0.0).astype(BF16)
        out_ref[ri * half:(ri + 1) * half, :] = jnp.where(same_ph, _dot(clc_ref[ri], rep_h), 0.0).astype(BF16)


def _s5_expand(kc, lbc, clc):
    half = S5_GPB * SSM_STATE
    return pl.pallas_call(
        _s5_expand_body, grid=(S5_BLOCKS, S5_CHUNK),
        in_specs=[pl.BlockSpec((None, S5_CHUNK, V7X_LANES, SSM_GROUP), lambda b, r: (b, 0, 0, 0)),
                  pl.BlockSpec((None, None, 2, V7X_LANES, SSM_STATE), lambda b, r: (b, r, 0, 0, 0)),
                  pl.BlockSpec((None, None, 2, half, SSM_GROUP), lambda b, r: (b, r, 0, 0, 0))],
        out_specs=[pl.BlockSpec((None, V7X_LANES, S5_WIDE), lambda b, r: (b, r, 0)),
                   pl.BlockSpec((None, V7X_LANES, 2 * half), lambda b, r: (b, r, 0)),
                   pl.BlockSpec((None, 2 * half, V7X_LANES), lambda b, r: (b, 0, r))],
        out_shape=[jax.ShapeDtypeStruct((S5_BLOCKS, S5_WIDE, S5_WIDE), BF16),
                   jax.ShapeDtypeStruct((S5_BLOCKS, S5_WIDE, 2 * half), BF16),
                   jax.ShapeDtypeStruct((S5_BLOCKS, 2 * half, S5_WIDE), BF16)],
        compiler_params=_params(2, 32 * 1024 * 1024), name="s5_expand")(kc, lbc, clc)


def _s5_body(u_ref, t_ref, b_ref, c_ref, a_ref, y_ref, s_ref, x_ref, xr_ref, xi_ref, *, nb):
    half = S5_GPB * SSM_STATE

    @pl.when(pl.program_id(1) == 0)
    def _():
        xr_ref[...] = jnp.zeros((nb, half), F32)
        xi_ref[...] = jnp.zeros((nb, half), F32)

    s_ref[...] = _dot(u_ref[...], b_ref[...])
    ar = a_ref[0:1, :]
    ai = a_ref[1:2, :]

    def step(c, carry):
        xr, xi = carry
        rows = pl.ds(pl.multiple_of(c * nb, nb), nb)
        x_ref[rows, 0:half] = xr.astype(BF16)
        x_ref[rows, half:] = xi.astype(BF16)
        return (ar * xr - ai * xi + s_ref[rows, 0:half], ar * xi + ai * xr + s_ref[rows, half:])

    xr, xi = lax.fori_loop(0, u_ref.shape[0] // nb, step, (xr_ref[...], xi_ref[...]))
    xr_ref[...] = xr
    xi_ref[...] = xi
    for nt in range(S5_WIDE // S5_COLS):
        cols = slice(nt * S5_COLS, (nt + 1) * S5_COLS)
        k = (nt + 1) * S5_COLS
        y_ref[:, cols] = _dot(u_ref[:, 0:k], t_ref[0:k, cols]) + _dot(x_ref[...], c_ref[:, cols])


def _s5(u8, tables, bsz, seq):
    nchunk = seq // S5_CHUNK
    rows = S5_ROWS_CHUNKS * bsz
    half = S5_GPB * SSM_STATE
    tab = lambda a, b: pl.BlockSpec((None, a, b), lambda g, c: (g, 0, 0), pipeline_mode=pl.Buffered(1))
    act = pl.BlockSpec((None, rows, S5_WIDE), lambda g, c: (g, c, 0))
    return pl.pallas_call(
        functools.partial(_s5_body, nb=bsz), grid=(S5_BLOCKS, nchunk // S5_ROWS_CHUNKS),
        in_specs=[act, tab(S5_WIDE, S5_WIDE), tab(S5_WIDE, 2 * half), tab(2 * half, S5_WIDE), tab(2, half)],
        out_specs=act, out_shape=jax.ShapeDtypeStruct((S5_BLOCKS, nchunk * bsz, S5_WIDE), F32),
        scratch_shapes=[pltpu.VMEM((rows, 2 * half), F32), pltpu.VMEM((rows, 2 * half), BF16),
                        pltpu.VMEM((bsz, half), F32), pltpu.VMEM((bsz, half), F32)],
        compiler_params=_params(2, 48 * 1024 * 1024), name="s5")(u8, *tables)


def _chunk_layout_spec():
    return pl.BlockSpec((S5_BLOCKS, ROW_TILE // S5_CHUNK, S5_WIDE), lambda b, l: (0, l, b))


def _mix_out_body(x_ref, yc_ref, y8_ref, u_ref, d_ref, gw_ref, gb_ref, wc_ref, ws_ref, o_ref, ys_ref):
    tl = x_ref.shape[0]
    for gb in range(S5_BLOCKS):
        for s in range(S5_CHUNK):
            ys_ref[gb, pl.ds(s, tl // S5_CHUNK, stride=S5_CHUNK), :] = (
                y8_ref[gb, :, s * V7X_LANES:(s + 1) * V7X_LANES])
    ys = jnp.concatenate([ys_ref[gb] for gb in range(S5_BLOCKS)], axis=1)
    y = ys + d_ref[...] * u_ref[...]
    y = 0.5 * y * (1.0 + jnp.tanh(math.sqrt(2.0 / math.pi) * (y + 0.044715 * (y * y * y))))
    gate = jax.nn.sigmoid(_dot(y.astype(BF16), gw_ref[...]) + gb_ref[...])
    y = (y * gate).astype(BF16)
    o_ref[...] = x_ref[...] + _dot(yc_ref[...], wc_ref[...]) + _dot(y, ws_ref[...])


def _mix_out(x, y_conv, y_state, u, d, glu_w, glu_b, w_out, bsz, seq):
    t = x.shape[0]
    nl = seq // ROW_TILE
    row = lambda w: pl.BlockSpec((ROW_TILE, w), lambda b, l: (b * nl + l, 0))
    return pl.pallas_call(
        _mix_out_body, grid=(bsz, nl),
        in_specs=[row(D_MODEL), row(CONV_CH), _chunk_layout_spec(), row(SSM_WIDTH), _resident((1, SSM_WIDTH)),
                  _resident((SSM_WIDTH, SSM_WIDTH)), _resident((1, SSM_WIDTH)),
                  _resident((CONV_CH, D_MODEL)), _resident((SSM_WIDTH, D_MODEL))],
        out_specs=row(D_MODEL), out_shape=jax.ShapeDtypeStruct((t, D_MODEL), F32),
        scratch_shapes=[pltpu.VMEM((S5_BLOCKS, ROW_TILE, V7X_LANES), F32)],
        compiler_params=_params(2, 40 * 1024 * 1024), name="mix_out")(
            x, y_conv, y_state, u, d.reshape(1, SSM_WIDTH), glu_w.astype(BF16),
            glu_b.reshape(1, SSM_WIDTH), w_out[:CONV_CH].astype(BF16), w_out[CONV_CH:].astype(BF16))


def _qkv_body(x_ref, g_ref, w_ref, qkv_ref, km_ref):
    h = _rms(x_ref[...], g_ref[...]).astype(BF16)
    r = _dot(h, w_ref[...])
    qkv_ref[...] = r.astype(BF16)
    for blk in range(x_ref.shape[0] // MOBA_BLOCK):
        k = r[blk * MOBA_BLOCK:(blk + 1) * MOBA_BLOCK, D_MODEL:2 * D_MODEL]
        km_ref[blk] = jnp.mean(k, axis=0, keepdims=True)


def _qkv(x, g, w_qkv):
    t = x.shape[0]
    nb = ROW_TILE // MOBA_BLOCK
    return pl.pallas_call(
        _qkv_body, grid=(t // ROW_TILE,),
        in_specs=[pl.BlockSpec((ROW_TILE, D_MODEL), lambda i: (i, 0)), _resident((1, D_MODEL)),
                  _resident((D_MODEL, 3 * D_MODEL))],
        out_specs=[pl.BlockSpec((ROW_TILE, 3 * D_MODEL), lambda i: (i, 0)),
                   pl.BlockSpec((nb, 1, D_MODEL), lambda i: (i, 0, 0))],
        out_shape=[jax.ShapeDtypeStruct((t, 3 * D_MODEL), BF16),
                   jax.ShapeDtypeStruct((t // MOBA_BLOCK, 1, D_MODEL), F32)],
        compiler_params=_params(1, 40 * 1024 * 1024), name="qkv")(
            x, g.reshape(1, D_MODEL), w_qkv.astype(BF16))


def _moba_body(q_ref, k_ref, v_ref, km_ref, o_ref, kaug_ref, vaug_ref):
    seq = q_ref.shape[0]
    nblk = seq // MOBA_BLOCK
    lane = lax.broadcasted_iota(jnp.int32, (seq, HEAD_DIM), 1)
    blk_of_row = lax.broadcasted_iota(jnp.int32, (seq, HEAD_DIM), 0) // MOBA_BLOCK
    kaug_ref[:, 0:HEAD_DIM] = k_ref[...]
    kaug_ref[:, HEAD_DIM:] = jnp.where(lane == blk_of_row, 1.0, 0.0).astype(BF16)
    vaug_ref[:, 0:HEAD_DIM] = v_ref[...]
    vaug_ref[:, HEAD_DIM:] = jnp.ones((seq, HEAD_DIM), BF16)
    km = km_ref[...].astype(BF16)
    blk_id = lax.broadcasted_iota(jnp.int32, (nblk, MOBA_BLOCK), 0)
    qi = lax.broadcasted_iota(jnp.int32, (MOBA_BLOCK, MOBA_BLOCK), 0)
    ki = lax.broadcasted_iota(jnp.int32, (MOBA_BLOCK, MOBA_BLOCK), 1)
    for i in range(nblk):
        own = slice(i * MOBA_BLOCK, (i + 1) * MOBA_BLOCK)
        q = q_ref[own, :]
        s_own = jnp.where(ki <= qi, _dot_nt(q, k_ref[own, :]), MASK_VALUE)
        m = jnp.max(s_own, axis=1, keepdims=True)
        if i > 0:
            past = blk_id < i
            if i > MOBA_TOPK:
                gate = _dot_nt(km, q)
                rank = jnp.zeros((nblk, MOBA_BLOCK), jnp.int32)
                for j in range(i):
                    gj = gate[j:j + 1, :]
                    ahead = jnp.where(gj > gate, 1, jnp.where(gj == gate, jnp.where(blk_id > j, 1, 0), 0))
                    rank = rank + ahead
                chosen = jnp.where(past, rank, MOBA_TOPK) < MOBA_TOPK
            else:
                chosen = past
            bias = jnp.where(chosen, 0.0, MASK_VALUE)
            bias = jnp.concatenate([bias, jnp.zeros((HEAD_DIM - nblk, MOBA_BLOCK), F32)], axis=0)
            q_aug = jnp.concatenate([q, bias.T.astype(BF16)], axis=1)
            s_past = _dot_nt(q_aug, kaug_ref[0:i * MOBA_BLOCK, :])
            m = jnp.maximum(m, jnp.max(s_past, axis=1, keepdims=True))
            p_past = jnp.exp2((s_past - m) * SOFTMAX_EXP2_SCALE).astype(BF16)
        p_own = jnp.exp2((s_own - m) * SOFTMAX_EXP2_SCALE).astype(BF16)
        o = _dot(p_own, vaug_ref[own, :])
        if i > 0:
            o = o + _dot(p_past, vaug_ref[0:i * MOBA_BLOCK, :])
        o_ref[own, :] = (o[:, :HEAD_DIM] / o[:, HEAD_DIM:]).astype(BF16)


def _moba(qkv, kmean, bsz, seq):
    nblk = seq // MOBA_BLOCK
    col = lambda off: pl.BlockSpec((seq, HEAD_DIM), lambda b, h: (b, off + h))
    return pl.pallas_call(
        _moba_body, grid=(bsz, N_HEADS),
        in_specs=[col(0), col(N_HEADS), col(2 * N_HEADS),
                  pl.BlockSpec((nblk, HEAD_DIM), lambda b, h: (b, h))],
        out_specs=col(0), out_shape=jax.ShapeDtypeStruct((bsz * seq, D_MODEL), BF16),
        scratch_shapes=[pltpu.VMEM((seq, 2 * HEAD_DIM), BF16), pltpu.VMEM((seq, 2 * HEAD_DIM), BF16)],
        compiler_params=_params(2, 40 * 1024 * 1024), name="moba")(qkv, qkv, qkv, kmean)


def _attn_out_body(x_ref, o_ref, w_ref, y_ref):
    y_ref[...] = x_ref[...] + _dot(o_ref[...], w_ref[...])


def _attn_out(x, o, w_o):
    t = x.shape[0]
    row = pl.BlockSpec((ROW_TILE, D_MODEL), lambda i: (i, 0))
    return pl.pallas_call(
        _attn_out_body, grid=(t // ROW_TILE,),
        in_specs=[row, row, _resident((D_MODEL, D_MODEL))],
        out_specs=row, out_shape=jax.ShapeDtypeStruct((t, D_MODEL), F32),
        compiler_params=_params(1, 32 * 1024 * 1024), name="attn_out")(x, o, w_o.astype(BF16))


def kernel(x, ffn_norm, ffn_w1, ffn_w3, ffn_w2, mix_norm, ab_w_in, conv_w, conv_b, conv_ln_g, conv_ln_b, ssm_a_re, ssm_a_im, ssm_b_re, ssm_b_im, ssm_c_re, ssm_c_im, ssm_d, ssm_log_dt, ssm_glu_w, ssm_glu_b, ab_w_out, attn_w_qkv, attn_w_o, final_norm):
    bsz, seq, d = x.shape
    depth = ffn_norm.shape[0]
    assert d == D_MODEL and seq % ROW_TILE == 0 and ROW_TILE % MOBA_BLOCK == 0
    assert bsz % (2 * V7X_SUBLANES) == 0, "the S5 boundary scan walks whole packed bf16 tiles of batch rows"
    x = x.reshape(bsz * seq, d)
    for l in range(depth):
        x = _ffn(x, ffn_norm[l, 0], ffn_w1[l, 0], ffn_w3[l, 0], ffn_w2[l, 0])
        if l % 2 == 0:
            e = l // 2
            y_conv, u, u8 = _mix_in(x, mix_norm[l], ab_w_in[e], conv_w[e], conv_b[e], conv_ln_g[e],
                                    conv_ln_b[e], bsz, seq)
            tables = _s5_tables(ssm_a_re[e], ssm_a_im[e], ssm_b_re[e], ssm_b_im[e], ssm_c_re[e],
                                ssm_c_im[e], ssm_log_dt[e])
            nchunk = seq // S5_CHUNK
            y8 = _s5(u8.reshape(S5_BLOCKS, nchunk * bsz, S5_WIDE), tables, bsz, seq)
            x = _mix_out(x, y_conv, y8.reshape(S5_BLOCKS, nchunk, bsz * S5_WIDE), u, ssm_d[e], ssm_glu_w[e],
                         ssm_glu_b[e], ab_w_out[e], bsz, seq)
        else:
            o = l // 2
            qkv, kmean = _qkv(x, mix_norm[l], attn_w_qkv[o])
            att = _moba(qkv, kmean.reshape(-1, D_MODEL), bsz, seq)
            x = _attn_out(x, att, attn_w_o[o])
        last = l == depth - 1
        x = _ffn(x, ffn_norm[l, 1], ffn_w1[l, 1], ffn_w3[l, 1], ffn_w2[l, 1],
                 final_g=final_norm if last else None)
    return x.reshape(bsz, seq, d)
```

```python
import functools
import math

import jax
import jax.numpy as jnp
from jax import lax
from jax.experimental import pallas as pl
from jax.experimental.pallas import tpu as pltpu

F32 = jnp.float32
BF16 = jnp.bfloat16

D_MODEL = 1024
D_FF = 2816
RMS_EPS = 1e-6
LN_EPS = 1e-5
CONV_CH = 512
CONV_TAPS = 31
SSM_WIDTH = 512
SSM_GROUP = 16
SSM_GROUPS = SSM_WIDTH // SSM_GROUP
SSM_STATE = 64
IN_WIDTH = 2 * CONV_CH + SSM_WIDTH
N_HEADS = 8
HEAD_DIM = 128
MOBA_BLOCK = 256
MOBA_TOPK = 3

V7X_LANES = 128
V7X_SUBLANES = 8
V7X_VMEM_BYTES = 64 * 1024 * 1024

ROW_TILE = 512
FF_CHUNK = 256
CONV_HALO = 32
CONV_ROWS = 128
S5_CHUNK = 16
S5_BLOCKS = SSM_WIDTH // V7X_LANES
S5_GPB = V7X_LANES // SSM_GROUP
S5_WIDE = S5_CHUNK * V7X_LANES
S5_COLS = 256
S5_ROWS_CHUNKS = 32
MASK_VALUE = -1e30
SOFTMAX_EXP2_SCALE = HEAD_DIM ** -0.5 * math.log2(math.e)


def _vmem_limit(n_bytes):
    return int(min(n_bytes, V7X_VMEM_BYTES - 8 * 1024 * 1024))


def _params(n_grid, vmem_bytes):
    return pltpu.CompilerParams(
        dimension_semantics=("arbitrary",) * n_grid, vmem_limit_bytes=_vmem_limit(vmem_bytes))


def _resident(shape):
    nd = len(shape)
    return pl.BlockSpec(shape, lambda *_: (0,) * nd, pipeline_mode=pl.Buffered(1))


def _rms(x, g):
    return x * lax.rsqrt(jnp.mean(x * x, axis=-1, keepdims=True) + RMS_EPS) * g


def _dot(a, b, **kw):
    return jnp.dot(a, b, preferred_element_type=F32, **kw)


def _dot_nt(a, b):
    return lax.dot_general(a, b, (((1,), (1,)), ((), ())), preferred_element_type=F32)


def _ffn_body(*refs, final, attn):
    x_ref, g_ref, w1_ref, w3_ref, w2_ref = refs[:5]
    o_ref = refs[-1]
    extra = list(refs[5:-1])
    x = x_ref[...]
    if attn:
        att_ref, wo_ref = extra[:2]
        extra = extra[2:]
        x = x + _dot(att_ref[...], wo_ref[...])
    if final:
        gf_ref, = extra
    h = _rms(x, g_ref[...]).astype(BF16)
    acc = None
    for c0 in range(0, D_FF, FF_CHUNK):
        a = _dot(h, w1_ref[:, c0:c0 + FF_CHUNK])
        b = _dot(h, w3_ref[:, c0:c0 + FF_CHUNK])
        act = (a * jax.nn.sigmoid(a) * b).astype(BF16)
        part = _dot(act, w2_ref[c0:c0 + FF_CHUNK, :])
        acc = part if acc is None else acc + part
    y = x + 0.5 * acc
    if final:
        y = _rms(y, gf_ref[...])
    o_ref[...] = y


def _ffn(x, g, w1, w3, w2, final_g=None, attn=None):
    t = x.shape[0]
    row = pl.BlockSpec((ROW_TILE, D_MODEL), lambda i: (i, 0))
    in_specs = [row, _resident((1, D_MODEL)), _resident((D_MODEL, D_FF)),
                _resident((D_MODEL, D_FF)), _resident((D_FF, D_MODEL))]
    args = [x, g.reshape(1, D_MODEL), w1.astype(BF16), w3.astype(BF16), w2.astype(BF16)]
    if attn is not None:
        in_specs += [row, _resident((D_MODEL, D_MODEL))]
        args += [attn[0], attn[1].astype(BF16)]
    if final_g is not None:
        in_specs.append(_resident((1, D_MODEL)))
        args.append(final_g.reshape(1, D_MODEL))
    return pl.pallas_call(
        functools.partial(_ffn_body, final=final_g is not None, attn=attn is not None),
        grid=(t // ROW_TILE,), in_specs=in_specs, out_specs=row,
        out_shape=jax.ShapeDtypeStruct((t, D_MODEL), F32),
        compiler_params=_params(1, 48 * 1024 * 1024), name="ffn")(*args)


def _mix_in_body(x_ref, g_ref, win_ref, cw_ref, cb_ref, lng_ref, lnb_ref, yc_ref, u_ref, u8_ref,
                 vpad_ref, vsh_ref, y_ref, ub_ref):
    tl = x_ref.shape[0]

    @pl.when(pl.program_id(1) == 0)
    def _():
        vpad_ref[0:CONV_HALO, :] = jnp.zeros((CONV_HALO, CONV_CH), F32)

    h = _rms(x_ref[...], g_ref[...]).astype(BF16)
    p = _dot(h, win_ref[...])
    u_ref[...] = p[:, 2 * CONV_CH:]
    vpad_ref[CONV_HALO:CONV_HALO + tl, :] = p[:, :CONV_CH] * jax.nn.sigmoid(p[:, CONV_CH:2 * CONV_CH])
    for gb in range(S5_BLOCKS):
        ub_ref[gb] = p[:, 2 * CONV_CH + gb * V7X_LANES:2 * CONV_CH + (gb + 1) * V7X_LANES]
        for s in range(S5_CHUNK):
            u8_ref[gb, :, s * V7X_LANES:(s + 1) * V7X_LANES] = ub_ref[
                gb, pl.ds(s, tl // S5_CHUNK, stride=S5_CHUNK), :].astype(BF16)
    first = CONV_HALO - (CONV_TAPS - 1)
    for s in range(V7X_SUBLANES):
        span = tl + (CONV_TAPS - 1 - s) // V7X_SUBLANES * V7X_SUBLANES
        vsh_ref[s, 0:span, :] = vpad_ref[first + s:first + s + span, :]
    vpad_ref[0:CONV_HALO, :] = vpad_ref[tl:tl + CONV_HALO, :]
    for lb in range(CONV_CH // V7X_LANES):
        lanes = slice(lb * V7X_LANES, (lb + 1) * V7X_LANES)
        for r0 in range(0, tl, CONV_ROWS):
            acc = jnp.zeros((CONV_ROWS, V7X_LANES), F32)
            for j in range(CONV_TAPS):
                q, s = divmod(j, V7X_SUBLANES)
                rows = slice(r0 + q * V7X_SUBLANES, r0 + q * V7X_SUBLANES + CONV_ROWS)
                acc = acc + cw_ref[j:j + 1, lanes] * vsh_ref[s, rows, lanes]
            y_ref[r0:r0 + CONV_ROWS, lanes] = acc + cb_ref[:, lanes]
    y = y_ref[...]
    mu = jnp.mean(y, axis=-1, keepdims=True)
    yc = y - mu
    var = jnp.mean(yc * yc, axis=-1, keepdims=True)
    yn = yc * lax.rsqrt(var + LN_EPS) * lng_ref[...] + lnb_ref[...]
    yc_ref[...] = (yn * jax.nn.sigmoid(yn)).astype(BF16)


def _mix_in(x, g, w_in, conv_w, conv_b, ln_g, ln_b, bsz, seq):
    t = x.shape[0]
    nl = seq // ROW_TILE
    row = lambda w: pl.BlockSpec((ROW_TILE, w), lambda b, l: (b * nl + l, 0))
    return pl.pallas_call(
        _mix_in_body, grid=(bsz, nl),
        in_specs=[row(D_MODEL), _resident((1, D_MODEL)), _resident((D_MODEL, IN_WIDTH)),
                  _resident((CONV_TAPS, CONV_CH)), _resident((1, CONV_CH)), _resident((1, CONV_CH)),
                  _resident((1, CONV_CH))],
        out_specs=[row(CONV_CH), row(SSM_WIDTH), _chunk_layout_spec()],
        out_shape=[jax.ShapeDtypeStruct((t, CONV_CH), BF16), jax.ShapeDtypeStruct((t, SSM_WIDTH), F32),
                   jax.ShapeDtypeStruct((S5_BLOCKS, seq // S5_CHUNK, bsz * S5_WIDE), BF16)],
        scratch_shapes=[pltpu.VMEM((ROW_TILE + CONV_HALO, CONV_CH), F32),
                        pltpu.VMEM((V7X_SUBLANES, ROW_TILE + CONV_HALO - V7X_SUBLANES, CONV_CH), F32),
                        pltpu.VMEM((ROW_TILE, CONV_CH), F32),
                        pltpu.VMEM((S5_BLOCKS, ROW_TILE, V7X_LANES), F32)],
        compiler_params=_params(2, 48 * 1024 * 1024), name="mix_in")(
            x, g.reshape(1, D_MODEL), w_in.astype(BF16), conv_w, conv_b.reshape(1, CONV_CH),
            ln_g.reshape(1, CONV_CH), ln_b.reshape(1, CONV_CH))


def _s5_tables(a_re, a_im, b_re, b_im, c_re, c_im, log_dt):
    dt = jnp.exp(log_dt.astype(F32))[:, None]
    a_re, a_im = a_re.astype(F32), a_im.astype(F32)
    steps = jnp.arange(S5_CHUNK + 1, dtype=F32)[:, None, None]
    mag = jnp.exp(steps * (dt * a_re)[None])
    ang = steps * (dt * a_im)[None]
    pw_re, pw_im = mag * jnp.cos(ang), mag * jnp.sin(ang)
    den = a_re * a_re + a_im * a_im
    nr, ni = pw_re[1] - 1.0, pw_im[1]
    q_re = (nr * a_re + ni * a_im) / den
    q_im = (ni * a_re - nr * a_im) / den
    bt_re, bt_im = b_re.astype(F32).transpose(0, 2, 1), b_im.astype(F32).transpose(0, 2, 1)
    ct_re, ct_im = c_re.astype(F32).transpose(0, 2, 1), c_im.astype(F32).transpose(0, 2, 1)
    bb_re = q_re[:, None, :] * bt_re - q_im[:, None, :] * bt_im
    bb_im = q_re[:, None, :] * bt_im + q_im[:, None, :] * bt_re
    lb_re = pw_re[:-1, :, None, :] * bb_re[None] - pw_im[:-1, :, None, :] * bb_im[None]
    lb_im = pw_re[:-1, :, None, :] * bb_im[None] + pw_im[:-1, :, None, :] * bb_re[None]
    hi = lax.Precision.HIGHEST
    k = (jnp.einsum("ghp,jgip->jgih", c_re, lb_re, precision=hi)
         - jnp.einsum("ghp,jgip->jgih", c_im, lb_im, precision=hi))
    cl_re = ct_re[None] * pw_re[1:, :, :, None] - ct_im[None] * pw_im[1:, :, :, None]
    cl_im = ct_re[None] * pw_im[1:, :, :, None] + ct_im[None] * pw_re[1:, :, :, None]

    def per_block(m):
        return m.reshape(S5_CHUNK, S5_BLOCKS, S5_GPB * m.shape[2], m.shape[3]).transpose(1, 0, 2, 3)

    kc = per_block(k).astype(BF16)
    lbc = jnp.stack([per_block(lb_re[::-1]), per_block(lb_im[::-1])], axis=2).astype(BF16)
    clc = jnp.stack([per_block(cl_re), per_block(-cl_im)], axis=2).astype(BF16)
    a_chunk = jnp.stack([pw_re[S5_CHUNK].reshape(S5_BLOCKS, -1), pw_im[S5_CHUNK].reshape(S5_BLOCKS, -1)], axis=1)
    return _s5_expand(kc, lbc, clc) + (a_chunk,)


def _s5_expand_body(kc_ref, lbc_ref, clc_ref, toep_ref, in_ref, out_ref):
    r = pl.program_id(1)
    half = S5_GPB * SSM_STATE
    iota = lambda shape, d: lax.broadcasted_iota(jnp.int32, shape, d)
    one_hot = lambda c: jnp.where(c, 1.0, 0.0).astype(BF16)
    shift_g, shift_p = SSM_GROUP.bit_length() - 1, SSM_STATE.bit_length() - 1
    rep_h = one_hot((iota((SSM_GROUP, V7X_LANES), 1) & (SSM_GROUP - 1)) == iota((SSM_GROUP, V7X_LANES), 0))
    rep_p = one_hot((iota((SSM_STATE, half), 1) & (SSM_STATE - 1)) == iota((SSM_STATE, half), 0))
    same_hh = (iota((V7X_LANES, V7X_LANES), 0) >> shift_g) == (iota((V7X_LANES, V7X_LANES), 1) >> shift_g)
    same_hp = (iota((V7X_LANES, half), 0) >> shift_g) == (iota((V7X_LANES, half), 1) >> shift_p)
    same_ph = (iota((half, V7X_LANES), 0) >> shift_p) == (iota((half, V7X_LANES), 1) >> shift_g)
    for s in range(S5_CHUNK):
        blk = _dot(kc_ref[jnp.maximum(s - r, 0)], rep_h)
        blk = jnp.where(same_hh, blk, 0.0) * jnp.where(s >= r, 1.0, 0.0)
        toep_ref[:, s * V7X_LANES:(s + 1) * V7X_LANES] = blk.astype(BF16)
    for ri in range(2):
        in_ref[:, ri * half:(ri + 1) * half] = jnp.where(same_hp, _dot(lbc_ref[ri], rep_p), 0.0).astype(BF16)
        out_ref[ri * half:(ri + 1) * half, :] = jnp.where(same_ph, _dot(clc_ref[ri], rep_h), 0.0).astype(BF16)


def _s5_expand(kc, lbc, clc):
    half = S5_GPB * SSM_STATE
    return pl.pallas_call(
        _s5_expand_body, grid=(S5_BLOCKS, S5_CHUNK),
        in_specs=[pl.BlockSpec((None, S5_CHUNK, V7X_LANES, SSM_GROUP), lambda b, r: (b, 0, 0, 0)),
                  pl.BlockSpec((None, None, 2, V7X_LANES, SSM_STATE), lambda b, r: (b, r, 0, 0, 0)),
                  pl.BlockSpec((None, None, 2, half, SSM_GROUP), lambda b, r: (b, r, 0, 0, 0))],
        out_specs=[pl.BlockSpec((None, V7X_LANES, S5_WIDE), lambda b, r: (b, r, 0)),
                   pl.BlockSpec((None, V7X_LANES, 2 * half), lambda b, r: (b, r, 0)),
                   pl.BlockSpec((None, 2 * half, V7X_LANES), lambda b, r: (b, 0, r))],
        out_shape=[jax.ShapeDtypeStruct((S5_BLOCKS, S5_WIDE, S5_WIDE), BF16),
                   jax.ShapeDtypeStruct((S5_BLOCKS, S5_WIDE, 2 * half), BF16),
                   jax.ShapeDtypeStruct((S5_BLOCKS, 2 * half, S5_WIDE), BF16)],
        compiler_params=_params(2, 32 * 1024 * 1024), name="s5_expand")(kc, lbc, clc)


def _s5_body(u_ref, t_ref, b_ref, c_ref, a_ref, y_ref, s_ref, x_ref, xr_ref, xi_ref, *, nb):
    half = S5_GPB * SSM_STATE

    @pl.when(pl.program_id(1) == 0)
    def _():
        xr_ref[...] = jnp.zeros((nb, half), F32)
        xi_ref[...] = jnp.zeros((nb, half), F32)

    s_ref[...] = _dot(u_ref[...], b_ref[...])
    ar = a_ref[0:1, :]
    ai = a_ref[1:2, :]

    def step(c, carry):
        xr, xi = carry
        rows = pl.ds(pl.multiple_of(c * nb, nb), nb)
        x_ref[rows, 0:half] = xr.astype(BF16)
        x_ref[rows, half:] = xi.astype(BF16)
        return (ar * xr - ai * xi + s_ref[rows, 0:half], ar * xi + ai * xr + s_ref[rows, half:])

    xr, xi = lax.fori_loop(0, u_ref.shape[0] // nb, step, (xr_ref[...], xi_ref[...]))
    xr_ref[...] = xr
    xi_ref[...] = xi
    for nt in range(S5_WIDE // S5_COLS):
        cols = slice(nt * S5_COLS, (nt + 1) * S5_COLS)
        k = (nt + 1) * S5_COLS
        y_ref[:, cols] = _dot(u_ref[:, 0:k], t_ref[0:k, cols]) + _dot(x_ref[...], c_ref[:, cols])


def _s5(u8, tables, bsz, seq):
    nchunk = seq // S5_CHUNK
    rows = S5_ROWS_CHUNKS * bsz
    half = S5_GPB * SSM_STATE
    tab = lambda a, b: pl.BlockSpec((None, a, b), lambda g, c: (g, 0, 0), pipeline_mode=pl.Buffered(1))
    act = pl.BlockSpec((None, rows, S5_WIDE), lambda g, c: (g, c, 0))
    return pl.pallas_call(
        functools.partial(_s5_body, nb=bsz), grid=(S5_BLOCKS, nchunk // S5_ROWS_CHUNKS),
        in_specs=[act, tab(S5_WIDE, S5_WIDE), tab(S5_WIDE, 2 * half), tab(2 * half, S5_WIDE), tab(2, half)],
        out_specs=act, out_shape=jax.ShapeDtypeStruct((S5_BLOCKS, nchunk * bsz, S5_WIDE), F32),
        scratch_shapes=[pltpu.VMEM((rows, 2 * half), F32), pltpu.VMEM((rows, 2 * half), BF16),
                        pltpu.VMEM((bsz, half), F32), pltpu.VMEM((bsz, half), F32)],
        compiler_params=_params(2, 48 * 1024 * 1024), name="s5")(u8, *tables)


def _chunk_layout_spec():
    return pl.BlockSpec((S5_BLOCKS, ROW_TILE // S5_CHUNK, S5_WIDE), lambda b, l: (0, l, b))


def _mix_out_body(x_ref, yc_ref, y8_ref, u_ref, d_ref, gw_ref, gb_ref, wc_ref, ws_ref, o_ref, ys_ref):
    tl = x_ref.shape[0]
    for gb in range(S5_BLOCKS):
        for s in range(S5_CHUNK):
            ys_ref[gb, pl.ds(s, tl // S5_CHUNK, stride=S5_CHUNK), :] = (
                y8_ref[gb, :, s * V7X_LANES:(s + 1) * V7X_LANES])
    ys = jnp.concatenate([ys_ref[gb] for gb in range(S5_BLOCKS)], axis=1)
    y = ys + d_ref[...] * u_ref[...]
    y = 0.5 * y * (1.0 + jnp.tanh(math.sqrt(2.0 / math.pi) * (y + 0.044715 * (y * y * y))))
    gate = jax.nn.sigmoid(_dot(y.astype(BF16), gw_ref[...]) + gb_ref[...])
    y = (y * gate).astype(BF16)
    o_ref[...] = x_ref[...] + _dot(yc_ref[...], wc_ref[...]) + _dot(y, ws_ref[...])


def _mix_out(x, y_conv, y_state, u, d, glu_w, glu_b, w_out, bsz, seq):
    t = x.shape[0]
    nl = seq // ROW_TILE
    row = lambda w: pl.BlockSpec((ROW_TILE, w), lambda b, l: (b * nl + l, 0))
    return pl.pallas_call(
        _mix_out_body, grid=(bsz, nl),
        in_specs=[row(D_MODEL), row(CONV_CH), _chunk_layout_spec(), row(SSM_WIDTH), _resident((1, SSM_WIDTH)),
                  _resident((SSM_WIDTH, SSM_WIDTH)), _resident((1, SSM_WIDTH)),
                  _resident((CONV_CH, D_MODEL)), _resident((SSM_WIDTH, D_MODEL))],
        out_specs=row(D_MODEL), out_shape=jax.ShapeDtypeStruct((t, D_MODEL), F32),
        scratch_shapes=[pltpu.VMEM((S5_BLOCKS, ROW_TILE, V7X_LANES), F32)],
        compiler_params=_params(2, 40 * 1024 * 1024), name="mix_out")(
            x, y_conv, y_state, u, d.reshape(1, SSM_WIDTH), glu_w.astype(BF16),
            glu_b.reshape(1, SSM_WIDTH), w_out[:CONV_CH].astype(BF16), w_out[CONV_CH:].astype(BF16))


def _qkv_body(x_ref, g_ref, w_ref, qkv_ref, km_ref):
    h = _rms(x_ref[...], g_ref[...]).astype(BF16)
    r = _dot(h, w_ref[...])
    qkv_ref[...] = r.astype(BF16)
    for blk in range(x_ref.shape[0] // MOBA_BLOCK):
        k = r[blk * MOBA_BLOCK:(blk + 1) * MOBA_BLOCK, D_MODEL:2 * D_MODEL]
        km_ref[blk] = jnp.mean(k, axis=0, keepdims=True)


def _qkv(x, g, w_qkv):
    t = x.shape[0]
    nb = ROW_TILE // MOBA_BLOCK
    return pl.pallas_call(
        _qkv_body, grid=(t // ROW_TILE,),
        in_specs=[pl.BlockSpec((ROW_TILE, D_MODEL), lambda i: (i, 0)), _resident((1, D_MODEL)),
                  _resident((D_MODEL, 3 * D_MODEL))],
        out_specs=[pl.BlockSpec((ROW_TILE, 3 * D_MODEL), lambda i: (i, 0)),
                   pl.BlockSpec((nb, 1, D_MODEL), lambda i: (i, 0, 0))],
        out_shape=[jax.ShapeDtypeStruct((t, 3 * D_MODEL), BF16),
                   jax.ShapeDtypeStruct((t // MOBA_BLOCK, 1, D_MODEL), F32)],
        compiler_params=_params(1, 40 * 1024 * 1024), name="qkv")(
            x, g.reshape(1, D_MODEL), w_qkv.astype(BF16))


def _moba_body(q_ref, k_ref, v_ref, km_ref, o_ref, kaug_ref, vaug_ref):
    seq = q_ref.shape[0]
    nblk = seq // MOBA_BLOCK
    lane = lax.broadcasted_iota(jnp.int32, (seq, HEAD_DIM), 1)
    blk_of_row = lax.broadcasted_iota(jnp.int32, (seq, HEAD_DIM), 0) // MOBA_BLOCK
    kaug_ref[:, 0:HEAD_DIM] = k_ref[...]
    kaug_ref[:, HEAD_DIM:] = jnp.where(lane == blk_of_row, 1.0, 0.0).astype(BF16)
    vaug_ref[:, 0:HEAD_DIM] = v_ref[...]
    vaug_ref[:, HEAD_DIM:] = jnp.ones((seq, HEAD_DIM), BF16)
    km = km_ref[...].astype(BF16)
    blk_id = lax.broadcasted_iota(jnp.int32, (nblk, MOBA_BLOCK), 0)
    qi = lax.broadcasted_iota(jnp.int32, (MOBA_BLOCK, MOBA_BLOCK), 0)
    ki = lax.broadcasted_iota(jnp.int32, (MOBA_BLOCK, MOBA_BLOCK), 1)
    for i in range(nblk):
        own = slice(i * MOBA_BLOCK, (i + 1) * MOBA_BLOCK)
        q = q_ref[own, :]
        s_own = jnp.where(ki <= qi, _dot_nt(q, k_ref[own, :]), MASK_VALUE)
        m = jnp.max(s_own, axis=1, keepdims=True)
        if i > 0:
            past = blk_id < i
            if i > MOBA_TOPK:
                gate = _dot_nt(km, q)
                rank = jnp.zeros((nblk, MOBA_BLOCK), jnp.int32)
                for j in range(i):
                    gj = gate[j:j + 1, :]
                    ahead = jnp.where(gj > gate, 1, jnp.where(gj == gate, jnp.where(blk_id > j, 1, 0), 0))
                    rank = rank + ahead
                chosen = jnp.where(past, rank, MOBA_TOPK) < MOBA_TOPK
            else:
                chosen = past
            bias = jnp.where(chosen, 0.0, MASK_VALUE)
            bias = jnp.concatenate([bias, jnp.zeros((HEAD_DIM - nblk, MOBA_BLOCK), F32)], axis=0)
            q_aug = jnp.concatenate([q, bias.T.astype(BF16)], axis=1)
            s_past = _dot_nt(q_aug, kaug_ref[0:i * MOBA_BLOCK, :])
            m = jnp.maximum(m, jnp.max(s_past, axis=1, keepdims=True))
            p_past = jnp.exp2((s_past - m) * SOFTMAX_EXP2_SCALE).astype(BF16)
        p_own = jnp.exp2((s_own - m) * SOFTMAX_EXP2_SCALE).astype(BF16)
        o = _dot(p_own, vaug_ref[own, :])
        if i > 0:
            o = o + _dot(p_past, vaug_ref[0:i * MOBA_BLOCK, :])
        o_ref[own, :] = (o[:, :HEAD_DIM] / o[:, HEAD_DIM:]).astype(BF16)


def _moba(qkv, kmean, bsz, seq):
    nblk = seq // MOBA_BLOCK
    col = lambda off: pl.BlockSpec((seq, HEAD_DIM), lambda b, h: (b, off + h))
    return pl.pallas_call(
        _moba_body, grid=(bsz, N_HEADS),
        in_specs=[col(0), col(N_HEADS), col(2 * N_HEADS),
                  pl.BlockSpec((nblk, HEAD_DIM), lambda b, h: (b, h))],
        out_specs=col(0), out_shape=jax.ShapeDtypeStruct((bsz * seq, D_MODEL), BF16),
        scratch_shapes=[pltpu.VMEM((seq, 2 * HEAD_DIM), BF16), pltpu.VMEM((seq, 2 * HEAD_DIM), BF16)],
        compiler_params=_params(2, 40 * 1024 * 1024), name="moba")(qkv, qkv, qkv, kmean)


def kernel(x, ffn_norm, ffn_w1, ffn_w3, ffn_w2, mix_norm, ab_w_in, conv_w, conv_b, conv_ln_g, conv_ln_b, ssm_a_re, ssm_a_im, ssm_b_re, ssm_b_im, ssm_c_re, ssm_c_im, ssm_d, ssm_log_dt, ssm_glu_w, ssm_glu_b, ab_w_out, attn_w_qkv, attn_w_o, final_norm):
    bsz, seq, d = x.shape
    depth = ffn_norm.shape[0]
    assert d == D_MODEL and seq % ROW_TILE == 0 and ROW_TILE % MOBA_BLOCK == 0
    assert bsz % (2 * V7X_SUBLANES) == 0, "the S5 boundary scan walks whole packed bf16 tiles of batch rows"
    x = x.reshape(bsz * seq, d)
    for l in range(depth):
        x = _ffn(x, ffn_norm[l, 0], ffn_w1[l, 0], ffn_w3[l, 0], ffn_w2[l, 0])
        if l % 2 == 0:
            e = l // 2
            y_conv, u, u8 = _mix_in(x, mix_norm[l], ab_w_in[e], conv_w[e], conv_b[e], conv_ln_g[e],
                                    conv_ln_b[e], bsz, seq)
            tables = _s5_tables(ssm_a_re[e], ssm_a_im[e], ssm_b_re[e], ssm_b_im[e], ssm_c_re[e],
                                ssm_c_im[e], ssm_log_dt[e])
            nchunk = seq // S5_CHUNK
            y8 = _s5(u8.reshape(S5_BLOCKS, nchunk * bsz, S5_WIDE), tables, bsz, seq)
            x = _mix_out(x, y_conv, y8.reshape(S5_BLOCKS, nchunk, bsz * S5_WIDE), u, ssm_d[e], ssm_glu_w[e],
                         ssm_glu_b[e], ab_w_out[e], bsz, seq)
        else:
            o = l // 2
            qkv, kmean = _qkv(x, mix_norm[l], attn_w_qkv[o])
            attn = (_moba(qkv, kmean.reshape(-1, D_MODEL), bsz, seq), attn_w_o[o])
        last = l == depth - 1
        x = _ffn(x, ffn_norm[l, 1], ffn_w1[l, 1], ffn_w3[l, 1], ffn_w2[l, 1],
                 final_g=final_norm if last else None, attn=attn if l % 2 else None)
    return x.reshape(bsz, seq, d)
```

```python
import functools
import math

import jax
import jax.numpy as jnp
from jax import lax
from jax.experimental import pallas as pl
from jax.experimental.pallas import tpu as pltpu

F32 = jnp.float32
BF16 = jnp.bfloat16

D_MODEL = 1024
D_FF = 2816
RMS_EPS = 1e-6
LN_EPS = 1e-5
CONV_CH = 512
CONV_TAPS = 31
SSM_WIDTH = 512
SSM_GROUP = 16
SSM_GROUPS = SSM_WIDTH // SSM_GROUP
SSM_STATE = 64
IN_WIDTH = 2 * CONV_CH + SSM_WIDTH
N_HEADS = 8
HEAD_DIM = 128
MOBA_BLOCK = 256
MOBA_TOPK = 3

V7X_LANES = 128
V7X_SUBLANES = 8
V7X_VMEM_BYTES = 64 * 1024 * 1024

ROW_TILE = 512
FF_CHUNK = 256
CONV_HALO = 32
CONV_ROWS = 128
S5_CHUNK = 16
S5_BLOCKS = SSM_WIDTH // V7X_LANES
S5_GPB = V7X_LANES // SSM_GROUP
S5_WIDE = S5_CHUNK * V7X_LANES
S5_COLS = 256
S5_ROWS_CHUNKS = 32
MOBA_HEADS_PER_STEP = 2
MASK_VALUE = -1e30
SOFTMAX_EXP2_SCALE = HEAD_DIM ** -0.5 * math.log2(math.e)


def _vmem_limit(n_bytes):
    return int(min(n_bytes, V7X_VMEM_BYTES - 8 * 1024 * 1024))


def _params(n_grid, vmem_bytes):
    return pltpu.CompilerParams(
        dimension_semantics=("arbitrary",) * n_grid, vmem_limit_bytes=_vmem_limit(vmem_bytes))


def _resident(shape):
    nd = len(shape)
    return pl.BlockSpec(shape, lambda *_: (0,) * nd, pipeline_mode=pl.Buffered(1))


def _rms(x, g):
    return x * lax.rsqrt(jnp.mean(x * x, axis=-1, keepdims=True) + RMS_EPS) * g


def _dot(a, b, **kw):
    return jnp.dot(a, b, preferred_element_type=F32, **kw)


def _dot_nt(a, b):
    return lax.dot_general(a, b, (((1,), (1,)), ((), ())), preferred_element_type=F32)


def _ffn_body(*refs, final, attn):
    x_ref, g_ref, w1_ref, w3_ref, w2_ref = refs[:5]
    o_ref = refs[-1]
    extra = list(refs[5:-1])
    x = x_ref[...]
    if attn:
        att_ref, wo_ref = extra[:2]
        extra = extra[2:]
        x = x + _dot(att_ref[...], wo_ref[...])
    if final:
        gf_ref, = extra
    h = _rms(x, g_ref[...]).astype(BF16)
    acc = None
    for c0 in range(0, D_FF, FF_CHUNK):
        a = _dot(h, w1_ref[:, c0:c0 + FF_CHUNK])
        b = _dot(h, w3_ref[:, c0:c0 + FF_CHUNK])
        act = (a * jax.nn.sigmoid(a) * b).astype(BF16)
        part = _dot(act, w2_ref[c0:c0 + FF_CHUNK, :])
        acc = part if acc is None else acc + part
    y = x + 0.5 * acc
    if final:
        y = _rms(y, gf_ref[...])
    o_ref[...] = y


def _ffn(x, g, w1, w3, w2, which, final_g=None, attn=None):
    t = x.shape[0]
    row = pl.BlockSpec((ROW_TILE, D_MODEL), lambda i: (i, 0))
    layer, half = which
    pick = lambda a, b: pl.BlockSpec((None, None, a, b), lambda i: (layer, half, 0, 0),
                                     pipeline_mode=pl.Buffered(1))
    in_specs = [row, _resident((1, D_MODEL)), pick(D_MODEL, D_FF), pick(D_MODEL, D_FF), pick(D_FF, D_MODEL)]
    args = [x, g.reshape(1, D_MODEL), w1, w3, w2]
    if attn is not None:
        in_specs += [row, _resident((D_MODEL, D_MODEL))]
        args += [attn[0], attn[1].astype(BF16)]
    if final_g is not None:
        in_specs.append(_resident((1, D_MODEL)))
        args.append(final_g.reshape(1, D_MODEL))
    return pl.pallas_call(
        functools.partial(_ffn_body, final=final_g is not None, attn=attn is not None),
        grid=(t // ROW_TILE,), in_specs=in_specs, out_specs=row,
        out_shape=jax.ShapeDtypeStruct((t, D_MODEL), F32),
        compiler_params=_params(1, 48 * 1024 * 1024), name="ffn")(*args)


def _mix_in_body(x_ref, g_ref, win_ref, cw_ref, cb_ref, lng_ref, lnb_ref, yc_ref, u_ref, u8_ref,
                 vpad_ref, vsh_ref, y_ref, ub_ref):
    tl = x_ref.shape[0]

    @pl.when(pl.program_id(1) == 0)
    def _():
        vpad_ref[0:CONV_HALO, :] = jnp.zeros((CONV_HALO, CONV_CH), F32)

    h = _rms(x_ref[...], g_ref[...]).astype(BF16)
    p = _dot(h, win_ref[...])
    u_ref[...] = p[:, 2 * CONV_CH:]
    vpad_ref[CONV_HALO:CONV_HALO + tl, :] = p[:, :CONV_CH] * jax.nn.sigmoid(p[:, CONV_CH:2 * CONV_CH])
    for gb in range(S5_BLOCKS):
        ub_ref[gb] = p[:, 2 * CONV_CH + gb * V7X_LANES:2 * CONV_CH + (gb + 1) * V7X_LANES]
        for s in range(S5_CHUNK):
            u8_ref[gb, :, s * V7X_LANES:(s + 1) * V7X_LANES] = ub_ref[
                gb, pl.ds(s, tl // S5_CHUNK, stride=S5_CHUNK), :].astype(BF16)
    first = CONV_HALO - (CONV_TAPS - 1)
    for s in range(V7X_SUBLANES):
        span = tl + (CONV_TAPS - 1 - s) // V7X_SUBLANES * V7X_SUBLANES
        vsh_ref[s, 0:span, :] = vpad_ref[first + s:first + s + span, :]
    vpad_ref[0:CONV_HALO, :] = vpad_ref[tl:tl + CONV_HALO, :]
    for lb in range(CONV_CH // V7X_LANES):
        lanes = slice(lb * V7X_LANES, (lb + 1) * V7X_LANES)
        for r0 in range(0, tl, CONV_ROWS):
            acc = jnp.zeros((CONV_ROWS, V7X_LANES), F32)
            for j in range(CONV_TAPS):
                q, s = divmod(j, V7X_SUBLANES)
                rows = slice(r0 + q * V7X_SUBLANES, r0 + q * V7X_SUBLANES + CONV_ROWS)
                acc = acc + cw_ref[j:j + 1, lanes] * vsh_ref[s, rows, lanes]
            y_ref[r0:r0 + CONV_ROWS, lanes] = acc + cb_ref[:, lanes]
    y = y_ref[...]
    mu = jnp.mean(y, axis=-1, keepdims=True)
    yc = y - mu
    var = jnp.mean(yc * yc, axis=-1, keepdims=True)
    yn = yc * lax.rsqrt(var + LN_EPS) * lng_ref[...] + lnb_ref[...]
    yc_ref[...] = (yn * jax.nn.sigmoid(yn)).astype(BF16)


def _mix_in(x, g, w_in, conv_w, conv_b, ln_g, ln_b, bsz, seq):
    t = x.shape[0]
    nl = seq // ROW_TILE
    row = lambda w: pl.BlockSpec((ROW_TILE, w), lambda b, l: (b * nl + l, 0))
    return pl.pallas_call(
        _mix_in_body, grid=(bsz, nl),
        in_specs=[row(D_MODEL), _resident((1, D_MODEL)), _resident((D_MODEL, IN_WIDTH)),
                  _resident((CONV_TAPS, CONV_CH)), _resident((1, CONV_CH)), _resident((1, CONV_CH)),
                  _resident((1, CONV_CH))],
        out_specs=[row(CONV_CH), row(SSM_WIDTH), _chunk_layout_spec()],
        out_shape=[jax.ShapeDtypeStruct((t, CONV_CH), BF16), jax.ShapeDtypeStruct((t, SSM_WIDTH), F32),
                   jax.ShapeDtypeStruct((S5_BLOCKS, seq // S5_CHUNK, bsz * S5_WIDE), BF16)],
        scratch_shapes=[pltpu.VMEM((ROW_TILE + CONV_HALO, CONV_CH), F32),
                        pltpu.VMEM((V7X_SUBLANES, ROW_TILE + CONV_HALO - V7X_SUBLANES, CONV_CH), F32),
                        pltpu.VMEM((ROW_TILE, CONV_CH), F32),
                        pltpu.VMEM((S5_BLOCKS, ROW_TILE, V7X_LANES), F32)],
        compiler_params=_params(2, 48 * 1024 * 1024), name="mix_in")(
            x, g.reshape(1, D_MODEL), w_in.astype(BF16), conv_w, conv_b.reshape(1, CONV_CH),
            ln_g.reshape(1, CONV_CH), ln_b.reshape(1, CONV_CH))


def _s5_tables(a_re, a_im, b_re, b_im, c_re, c_im, log_dt):
    dt = jnp.exp(log_dt.astype(F32))[:, None]
    a_re, a_im = a_re.astype(F32), a_im.astype(F32)
    steps = jnp.arange(S5_CHUNK + 1, dtype=F32)[:, None, None]
    mag = jnp.exp(steps * (dt * a_re)[None])
    ang = steps * (dt * a_im)[None]
    pw_re, pw_im = mag * jnp.cos(ang), mag * jnp.sin(ang)
    den = a_re * a_re + a_im * a_im
    nr, ni = pw_re[1] - 1.0, pw_im[1]
    q_re = (nr * a_re + ni * a_im) / den
    q_im = (ni * a_re - nr * a_im) / den
    bt_re, bt_im = b_re.astype(F32).transpose(0, 2, 1), b_im.astype(F32).transpose(0, 2, 1)
    ct_re, ct_im = c_re.astype(F32).transpose(0, 2, 1), c_im.astype(F32).transpose(0, 2, 1)
    bb_re = q_re[:, None, :] * bt_re - q_im[:, None, :] * bt_im
    bb_im = q_re[:, None, :] * bt_im + q_im[:, None, :] * bt_re
    lb_re = pw_re[:-1, :, None, :] * bb_re[None] - pw_im[:-1, :, None, :] * bb_im[None]
    lb_im = pw_re[:-1, :, None, :] * bb_im[None] + pw_im[:-1, :, None, :] * bb_re[None]
    hi = lax.Precision.HIGHEST
    k = (jnp.einsum("ghp,jgip->jgih", c_re, lb_re, precision=hi)
         - jnp.einsum("ghp,jgip->jgih", c_im, lb_im, precision=hi))
    cl_re = ct_re[None] * pw_re[1:, :, :, None] - ct_im[None] * pw_im[1:, :, :, None]
    cl_im = ct_re[None] * pw_im[1:, :, :, None] + ct_im[None] * pw_re[1:, :, :, None]

    def per_block(m):
        return m.reshape(S5_CHUNK, S5_BLOCKS, S5_GPB * m.shape[2], m.shape[3]).transpose(1, 0, 2, 3)

    kc = per_block(k).astype(BF16)
    lbc = jnp.stack([per_block(lb_re[::-1]), per_block(lb_im[::-1])], axis=2).astype(BF16)
    clc = jnp.stack([per_block(cl_re), per_block(-cl_im)], axis=2).astype(BF16)
    a_chunk = jnp.stack([pw_re[S5_CHUNK].reshape(S5_BLOCKS, -1), pw_im[S5_CHUNK].reshape(S5_BLOCKS, -1)], axis=1)
    return _s5_expand(kc, lbc, clc) + (a_chunk,)


def _s5_expand_body(kc_ref, lbc_ref, clc_ref, toep_ref, in_ref, out_ref):
    r = pl.program_id(1)
    half = S5_GPB * SSM_STATE
    iota = lambda shape, d: lax.broadcasted_iota(jnp.int32, shape, d)
    one_hot = lambda c: jnp.where(c, 1.0, 0.0).astype(BF16)
    shift_g, shift_p = SSM_GROUP.bit_length() - 1, SSM_STATE.bit_length() - 1
    rep_h = one_hot((iota((SSM_GROUP, V7X_LANES), 1) & (SSM_GROUP - 1)) == iota((SSM_GROUP, V7X_LANES), 0))
    rep_p = one_hot((iota((SSM_STATE, half), 1) & (SSM_STATE - 1)) == iota((SSM_STATE, half), 0))
    same_hh = (iota((V7X_LANES, V7X_LANES), 0) >> shift_g) == (iota((V7X_LANES, V7X_LANES), 1) >> shift_g)
    same_hp = (iota((V7X_LANES, half), 0) >> shift_g) == (iota((V7X_LANES, half), 1) >> shift_p)
    same_ph = (iota((half, V7X_LANES), 0) >> shift_p) == (iota((half, V7X_LANES), 1) >> shift_g)
    for s in range(S5_CHUNK):
        blk = _dot(kc_ref[jnp.maximum(s - r, 0)], rep_h)
        blk = jnp.where(same_hh, blk, 0.0) * jnp.where(s >= r, 1.0, 0.0)
        toep_ref[:, s * V7X_LANES:(s + 1) * V7X_LANES] = blk.astype(BF16)
    for ri in range(2):
        in_ref[:, ri * half:(ri + 1) * half] = jnp.where(same_hp, _dot(lbc_ref[ri], rep_p), 0.0).astype(BF16)
        out_ref[ri * half:(ri + 1) * half, :] = jnp.where(same_ph, _dot(clc_ref[ri], rep_h), 0.0).astype(BF16)


def _s5_expand(kc, lbc, clc):
    half = S5_GPB * SSM_STATE
    return pl.pallas_call(
        _s5_expand_body, grid=(S5_BLOCKS, S5_CHUNK),
        in_specs=[pl.BlockSpec((None, S5_CHUNK, V7X_LANES, SSM_GROUP), lambda b, r: (b, 0, 0, 0)),
                  pl.BlockSpec((None, None, 2, V7X_LANES, SSM_STATE), lambda b, r: (b, r, 0, 0, 0)),
                  pl.BlockSpec((None, None, 2, half, SSM_GROUP), lambda b, r: (b, r, 0, 0, 0))],
        out_specs=[pl.BlockSpec((None, V7X_LANES, S5_WIDE), lambda b, r: (b, r, 0)),
                   pl.BlockSpec((None, V7X_LANES, 2 * half), lambda b, r: (b, r, 0)),
                   pl.BlockSpec((None, 2 * half, V7X_LANES), lambda b, r: (b, 0, r))],
        out_shape=[jax.ShapeDtypeStruct((S5_BLOCKS, S5_WIDE, S5_WIDE), BF16),
                   jax.ShapeDtypeStruct((S5_BLOCKS, S5_WIDE, 2 * half), BF16),
                   jax.ShapeDtypeStruct((S5_BLOCKS, 2 * half, S5_WIDE), BF16)],
        compiler_params=_params(2, 32 * 1024 * 1024), name="s5_expand")(kc, lbc, clc)


def _s5_body(u_ref, t_ref, b_ref, c_ref, a_ref, y_ref, s_ref, x_ref, xr_ref, xi_ref, *, nb):
    half = S5_GPB * SSM_STATE

    @pl.when(pl.program_id(1) == 0)
    def _():
        xr_ref[...] = jnp.zeros((nb, half), F32)
        xi_ref[...] = jnp.zeros((nb, half), F32)

    s_ref[...] = _dot(u_ref[...], b_ref[...])
    ar = a_ref[0:1, :]
    ai = a_ref[1:2, :]

    def step(c, carry):
        xr, xi = carry
        rows = pl.ds(pl.multiple_of(c * nb, nb), nb)
        x_ref[rows, 0:half] = xr.astype(BF16)
        x_ref[rows, half:] = xi.astype(BF16)
        return (ar * xr - ai * xi + s_ref[rows, 0:half], ar * xi + ai * xr + s_ref[rows, half:])

    xr, xi = lax.fori_loop(0, u_ref.shape[0] // nb, step, (xr_ref[...], xi_ref[...]))
    xr_ref[...] = xr
    xi_ref[...] = xi
    for nt in range(S5_WIDE // S5_COLS):
        cols = slice(nt * S5_COLS, (nt + 1) * S5_COLS)
        k = (nt + 1) * S5_COLS
        y_ref[:, cols] = _dot(u_ref[:, 0:k], t_ref[0:k, cols]) + _dot(x_ref[...], c_ref[:, cols])


def _s5(u8, tables, bsz, seq):
    nchunk = seq // S5_CHUNK
    rows = S5_ROWS_CHUNKS * bsz
    half = S5_GPB * SSM_STATE
    tab = lambda a, b: pl.BlockSpec((None, a, b), lambda g, c: (g, 0, 0), pipeline_mode=pl.Buffered(1))
    act = pl.BlockSpec((None, rows, S5_WIDE), lambda g, c: (g, c, 0))
    return pl.pallas_call(
        functools.partial(_s5_body, nb=bsz), grid=(S5_BLOCKS, nchunk // S5_ROWS_CHUNKS),
        in_specs=[act, tab(S5_WIDE, S5_WIDE), tab(S5_WIDE, 2 * half), tab(2 * half, S5_WIDE), tab(2, half)],
        out_specs=act, out_shape=jax.ShapeDtypeStruct((S5_BLOCKS, nchunk * bsz, S5_WIDE), F32),
        scratch_shapes=[pltpu.VMEM((rows, 2 * half), F32), pltpu.VMEM((rows, 2 * half), BF16),
                        pltpu.VMEM((bsz, half), F32), pltpu.VMEM((bsz, half), F32)],
        compiler_params=_params(2, 48 * 1024 * 1024), name="s5")(u8, *tables)


def _chunk_layout_spec():
    return pl.BlockSpec((S5_BLOCKS, ROW_TILE // S5_CHUNK, S5_WIDE), lambda b, l: (0, l, b))


def _mix_out_body(x_ref, yc_ref, y8_ref, u_ref, d_ref, gw_ref, gb_ref, wc_ref, ws_ref, o_ref, ys_ref):
    tl = x_ref.shape[0]
    for gb in range(S5_BLOCKS):
        for s in range(S5_CHUNK):
            ys_ref[gb, pl.ds(s, tl // S5_CHUNK, stride=S5_CHUNK), :] = (
                y8_ref[gb, :, s * V7X_LANES:(s + 1) * V7X_LANES])
    ys = jnp.concatenate([ys_ref[gb] for gb in range(S5_BLOCKS)], axis=1)
    y = ys + d_ref[...] * u_ref[...]
    y = 0.5 * y * (1.0 + jnp.tanh(math.sqrt(2.0 / math.pi) * (y + 0.044715 * (y * y * y))))
    gate = jax.nn.sigmoid(_dot(y.astype(BF16), gw_ref[...]) + gb_ref[...])
    y = (y * gate).astype(BF16)
    o_ref[...] = x_ref[...] + _dot(yc_ref[...], wc_ref[...]) + _dot(y, ws_ref[...])


def _mix_out(x, y_conv, y_state, u, d, glu_w, glu_b, w_out, bsz, seq):
    t = x.shape[0]
    nl = seq // ROW_TILE
    row = lambda w: pl.BlockSpec((ROW_TILE, w), lambda b, l: (b * nl + l, 0))
    return pl.pallas_call(
        _mix_out_body, grid=(bsz, nl),
        in_specs=[row(D_MODEL), row(CONV_CH), _chunk_layout_spec(), row(SSM_WIDTH), _resident((1, SSM_WIDTH)),
                  _resident((SSM_WIDTH, SSM_WIDTH)), _resident((1, SSM_WIDTH)),
                  _resident((CONV_CH, D_MODEL)), _resident((SSM_WIDTH, D_MODEL))],
        out_specs=row(D_MODEL), out_shape=jax.ShapeDtypeStruct((t, D_MODEL), F32),
        scratch_shapes=[pltpu.VMEM((S5_BLOCKS, ROW_TILE, V7X_LANES), F32)],
        compiler_params=_params(2, 40 * 1024 * 1024), name="mix_out")(
            x, y_conv, y_state, u, d.reshape(1, SSM_WIDTH), glu_w.astype(BF16),
            glu_b.reshape(1, SSM_WIDTH), w_out[:CONV_CH].astype(BF16), w_out[CONV_CH:].astype(BF16))


def _qkv_body(x_ref, g_ref, w_ref, qkv_ref, km_ref):
    h = _rms(x_ref[...], g_ref[...]).astype(BF16)
    r = _dot(h, w_ref[...])
    qkv_ref[...] = r.astype(BF16)
    for blk in range(x_ref.shape[0] // MOBA_BLOCK):
        k = r[blk * MOBA_BLOCK:(blk + 1) * MOBA_BLOCK, D_MODEL:2 * D_MODEL]
        km_ref[blk] = jnp.mean(k, axis=0, keepdims=True)


def _qkv(x, g, w_qkv):
    t = x.shape[0]
    nb = ROW_TILE // MOBA_BLOCK
    return pl.pallas_call(
        _qkv_body, grid=(t // ROW_TILE,),
        in_specs=[pl.BlockSpec((ROW_TILE, D_MODEL), lambda i: (i, 0)), _resident((1, D_MODEL)),
                  _resident((D_MODEL, 3 * D_MODEL))],
        out_specs=[pl.BlockSpec((ROW_TILE, 3 * D_MODEL), lambda i: (i, 0)),
                   pl.BlockSpec((nb, 1, D_MODEL), lambda i: (i, 0, 0))],
        out_shape=[jax.ShapeDtypeStruct((t, 3 * D_MODEL), BF16),
                   jax.ShapeDtypeStruct((t // MOBA_BLOCK, 1, D_MODEL), F32)],
        compiler_params=_params(1, 40 * 1024 * 1024), name="qkv")(
            x, g.reshape(1, D_MODEL), w_qkv.astype(BF16))


def _moba_body(q_ref, k_ref, v_ref, km_ref, o_ref, kaug_ref, vaug_ref):
    seq = q_ref.shape[0]
    nblk = seq // MOBA_BLOCK
    lane = lax.broadcasted_iota(jnp.int32, (seq, HEAD_DIM), 1)
    blk_of_row = lax.broadcasted_iota(jnp.int32, (seq, HEAD_DIM), 0) // MOBA_BLOCK
    block_one_hot = jnp.where(lane == blk_of_row, 1.0, 0.0).astype(BF16)
    for hh in range(MOBA_HEADS_PER_STEP):
        head = slice(hh * HEAD_DIM, (hh + 1) * HEAD_DIM)
        kaug_ref[hh, :, 0:HEAD_DIM] = k_ref[:, head]
        kaug_ref[hh, :, HEAD_DIM:] = block_one_hot
        vaug_ref[hh, :, 0:HEAD_DIM] = v_ref[:, head]
        vaug_ref[hh, :, HEAD_DIM:] = jnp.ones((seq, HEAD_DIM), BF16)
    blk_id = lax.broadcasted_iota(jnp.int32, (nblk, MOBA_BLOCK), 0)
    qi = lax.broadcasted_iota(jnp.int32, (MOBA_BLOCK, MOBA_BLOCK), 0)
    ki = lax.broadcasted_iota(jnp.int32, (MOBA_BLOCK, MOBA_BLOCK), 1)
    for i, hh in [(i, hh) for i in range(nblk) for hh in range(MOBA_HEADS_PER_STEP)]:
        head = slice(hh * HEAD_DIM, (hh + 1) * HEAD_DIM)
        kaug, vaug = kaug_ref.at[hh], vaug_ref.at[hh]
        own = slice(i * MOBA_BLOCK, (i + 1) * MOBA_BLOCK)
        q = q_ref[own, head]
        s_own = jnp.where(ki <= qi, _dot_nt(q, k_ref[own, head]), MASK_VALUE)
        m = jnp.max(s_own, axis=1, keepdims=True)
        if i > 0:
            past = blk_id < i
            if i > MOBA_TOPK:
                gate = _dot_nt(km_ref[:, head].astype(BF16), q)
                rank = jnp.zeros((nblk, MOBA_BLOCK), jnp.int32)
                for j in range(i):
                    gj = gate[j:j + 1, :]
                    ahead = jnp.where(gj > gate, 1, jnp.where(gj == gate, jnp.where(blk_id > j, 1, 0), 0))
                    rank = rank + ahead
                chosen = jnp.where(past, rank, MOBA_TOPK) < MOBA_TOPK
            else:
                chosen = past
            bias = jnp.where(chosen, 0.0, MASK_VALUE)
            bias = jnp.concatenate([bias, jnp.zeros((HEAD_DIM - nblk, MOBA_BLOCK), F32)], axis=0)
            q_aug = jnp.concatenate([q, bias.T.astype(BF16)], axis=1)
            s_past = _dot_nt(q_aug, kaug[0:i * MOBA_BLOCK, :])
            m = jnp.maximum(m, jnp.max(s_past, axis=1, keepdims=True))
            p_past = jnp.exp2((s_past - m) * SOFTMAX_EXP2_SCALE).astype(BF16)
        p_own = jnp.exp2((s_own - m) * SOFTMAX_EXP2_SCALE).astype(BF16)
        o = _dot(p_own, vaug[own, :])
        if i > 0:
            o = o + _dot(p_past, vaug[0:i * MOBA_BLOCK, :])
        o_ref[own, head] = (o[:, :HEAD_DIM] / o[:, HEAD_DIM:]).astype(BF16)


def _moba(qkv, kmean, bsz, seq):
    nblk = seq // MOBA_BLOCK
    width = MOBA_HEADS_PER_STEP * HEAD_DIM
    steps = N_HEADS // MOBA_HEADS_PER_STEP
    col = lambda off: pl.BlockSpec((seq, width), lambda b, h: (b, off + h))
    aug = pltpu.VMEM((MOBA_HEADS_PER_STEP, seq, 2 * HEAD_DIM), BF16)
    return pl.pallas_call(
        _moba_body, grid=(bsz, steps),
        in_specs=[col(0), col(steps), col(2 * steps), pl.BlockSpec((nblk, width), lambda b, h: (b, h))],
        out_specs=col(0), out_shape=jax.ShapeDtypeStruct((bsz * seq, D_MODEL), BF16),
        scratch_shapes=[aug, aug],
        compiler_params=_params(2, 40 * 1024 * 1024), name="moba")(qkv, qkv, qkv, kmean)


def kernel(x, ffn_norm, ffn_w1, ffn_w3, ffn_w2, mix_norm, ab_w_in, conv_w, conv_b, conv_ln_g, conv_ln_b, ssm_a_re, ssm_a_im, ssm_b_re, ssm_b_im, ssm_c_re, ssm_c_im, ssm_d, ssm_log_dt, ssm_glu_w, ssm_glu_b, ab_w_out, attn_w_qkv, attn_w_o, final_norm):
    bsz, seq, d = x.shape
    depth = ffn_norm.shape[0]
    assert d == D_MODEL and seq % ROW_TILE == 0 and ROW_TILE % MOBA_BLOCK == 0
    assert bsz % (2 * V7X_SUBLANES) == 0, "the S5 boundary scan walks whole packed bf16 tiles of batch rows"
    x = x.reshape(bsz * seq, d)
    w1, w3, w2 = ffn_w1.astype(BF16), ffn_w3.astype(BF16), ffn_w2.astype(BF16)
    for l in range(depth):
        x = _ffn(x, ffn_norm[l, 0], w1, w3, w2, (l, 0))
        if l % 2 == 0:
            e = l // 2
            y_conv, u, u8 = _mix_in(x, mix_norm[l], ab_w_in[e], conv_w[e], conv_b[e], conv_ln_g[e],
                                    conv_ln_b[e], bsz, seq)
            tables = _s5_tables(ssm_a_re[e], ssm_a_im[e], ssm_b_re[e], ssm_b_im[e], ssm_c_re[e],
                                ssm_c_im[e], ssm_log_dt[e])
            nchunk = seq // S5_CHUNK
            y8 = _s5(u8.reshape(S5_BLOCKS, nchunk * bsz, S5_WIDE), tables, bsz, seq)
            x = _mix_out(x, y_conv, y8.reshape(S5_BLOCKS, nchunk, bsz * S5_WIDE), u, ssm_d[e], ssm_glu_w[e],
                         ssm_glu_b[e], ab_w_out[e], bsz, seq)
        else:
            o = l // 2
            qkv, kmean = _qkv(x, mix_norm[l], attn_w_qkv[o])
            attn = (_moba(qkv, kmean.reshape(-1, D_MODEL), bsz, seq), attn_w_o[o])
        last = l == depth - 1
        x = _ffn(x, ffn_norm[l, 1], w1, w3, w2, (l, 1),
                 final_g=final_norm if last else None, attn=attn if l % 2 else None)
    return x.reshape(bsz, seq, d)
```

```python
import functools
import math

import jax
import jax.numpy as jnp
from jax import lax
from jax.experimental import pallas as pl
from jax.experimental.pallas import tpu as pltpu

F32 = jnp.float32
BF16 = jnp.bfloat16

D_MODEL = 1024
D_FF = 2816
RMS_EPS = 1e-6
LN_EPS = 1e-5
CONV_CH = 512
CONV_TAPS = 31
SSM_WIDTH = 512
SSM_GROUP = 16
SSM_GROUPS = SSM_WIDTH // SSM_GROUP
SSM_STATE = 64
IN_WIDTH = 2 * CONV_CH + SSM_WIDTH
N_HEADS = 8
HEAD_DIM = 128
MOBA_BLOCK = 256
MOBA_TOPK = 3

V7X_LANES = 128
V7X_SUBLANES = 8
V7X_VMEM_BYTES = 64 * 1024 * 1024

ROW_TILE = 512
FF_CHUNK = 256
CONV_HALO = 32
CONV_ROWS = 128
S5_CHUNK = 16
S5_BLOCKS = SSM_WIDTH // V7X_LANES
S5_GPB = V7X_LANES // SSM_GROUP
S5_WIDE = S5_CHUNK * V7X_LANES
S5_COLS = 256
S5_ROWS_CHUNKS = 32
S5_EXPAND_STEPS = 4
MOBA_HEADS_PER_STEP = 2
MASK_VALUE = -1e30
SOFTMAX_EXP2_SCALE = HEAD_DIM ** -0.5 * math.log2(math.e)


def _vmem_limit(n_bytes):
    return int(min(n_bytes, V7X_VMEM_BYTES - 8 * 1024 * 1024))


def _params(n_grid, vmem_bytes):
    return pltpu.CompilerParams(
        dimension_semantics=("arbitrary",) * n_grid, vmem_limit_bytes=_vmem_limit(vmem_bytes))


def _resident(shape):
    nd = len(shape)
    return pl.BlockSpec(shape, lambda *_: (0,) * nd, pipeline_mode=pl.Buffered(1))


def _rms(x, g):
    return x * lax.rsqrt(jnp.mean(x * x, axis=-1, keepdims=True) + RMS_EPS) * g


def _dot(a, b, **kw):
    return jnp.dot(a, b, preferred_element_type=F32, **kw)


def _dot_nt(a, b):
    return lax.dot_general(a, b, (((1,), (1,)), ((), ())), preferred_element_type=F32)


def _half_ffn(x, g_ref, w1_ref, w3_ref, w2_ref):
    h = _rms(x, g_ref[...]).astype(BF16)
    acc = None
    for c0 in range(0, D_FF, FF_CHUNK):
        a = _dot(h, w1_ref[:, c0:c0 + FF_CHUNK])
        b = _dot(h, w3_ref[:, c0:c0 + FF_CHUNK])
        act = (a * jax.nn.sigmoid(a) * b).astype(BF16)
        part = _dot(act, w2_ref[c0:c0 + FF_CHUNK, :])
        acc = part if acc is None else acc + part
    return x + 0.5 * acc


def _ffn_weight_specs(which):
    layer, half = which
    pick = lambda a, b: pl.BlockSpec((None, None, a, b), lambda *_: (layer, half, 0, 0),
                                     pipeline_mode=pl.Buffered(1))
    return [_resident((1, D_MODEL)), pick(D_MODEL, D_FF), pick(D_MODEL, D_FF), pick(D_FF, D_MODEL)]


def _ffn_body(*refs, final, attn):
    x_ref, g_ref, w1_ref, w3_ref, w2_ref = refs[:5]
    o_ref = refs[-1]
    extra = list(refs[5:-1])
    x = x_ref[...]
    if attn:
        att_ref, wo_ref = extra[:2]
        extra = extra[2:]
        x = x + _dot(att_ref[...], wo_ref[...])
    if final:
        gf_ref, = extra
    y = _half_ffn(x, g_ref, w1_ref, w3_ref, w2_ref)
    if final:
        y = _rms(y, gf_ref[...])
    o_ref[...] = y


def _ffn(x, g, w1, w3, w2, which, final_g=None, attn=None):
    t = x.shape[0]
    row = pl.BlockSpec((ROW_TILE, D_MODEL), lambda i: (i, 0))
    in_specs = [row] + _ffn_weight_specs(which)
    args = [x, g.reshape(1, D_MODEL), w1, w3, w2]
    if attn is not None:
        in_specs += [row, _resident((D_MODEL, D_MODEL))]
        args += [attn[0], attn[1].astype(BF16)]
    if final_g is not None:
        in_specs.append(_resident((1, D_MODEL)))
        args.append(final_g.reshape(1, D_MODEL))
    return pl.pallas_call(
        functools.partial(_ffn_body, final=final_g is not None, attn=attn is not None),
        grid=(t // ROW_TILE,), in_specs=in_specs, out_specs=row,
        out_shape=jax.ShapeDtypeStruct((t, D_MODEL), F32),
        compiler_params=_params(1, 48 * 1024 * 1024), name="ffn")(*args)


def _mix_in_body(x_ref, g_ref, win_ref, cw_ref, cb_ref, lng_ref, lnb_ref, yc_ref, u_ref, u8_ref,
                 vpad_ref, vsh_ref, y_ref, ub_ref):
    tl = x_ref.shape[0]

    @pl.when(pl.program_id(1) == 0)
    def _():
        vpad_ref[0:CONV_HALO, :] = jnp.zeros((CONV_HALO, CONV_CH), F32)

    h = _rms(x_ref[...], g_ref[...]).astype(BF16)
    p = _dot(h, win_ref[...])
    u_ref[...] = p[:, 2 * CONV_CH:]
    vpad_ref[CONV_HALO:CONV_HALO + tl, :] = p[:, :CONV_CH] * jax.nn.sigmoid(p[:, CONV_CH:2 * CONV_CH])
    for gb in range(S5_BLOCKS):
        ub_ref[gb] = p[:, 2 * CONV_CH + gb * V7X_LANES:2 * CONV_CH + (gb + 1) * V7X_LANES]
        for s in range(S5_CHUNK):
            u8_ref[gb, :, s * V7X_LANES:(s + 1) * V7X_LANES] = ub_ref[
                gb, pl.ds(s, tl // S5_CHUNK, stride=S5_CHUNK), :].astype(BF16)
    first = CONV_HALO - (CONV_TAPS - 1)
    for s in range(V7X_SUBLANES):
        span = tl + (CONV_TAPS - 1 - s) // V7X_SUBLANES * V7X_SUBLANES
        vsh_ref[s, 0:span, :] = vpad_ref[first + s:first + s + span, :]
    vpad_ref[0:CONV_HALO, :] = vpad_ref[tl:tl + CONV_HALO, :]
    for lb in range(CONV_CH // V7X_LANES):
        lanes = slice(lb * V7X_LANES, (lb + 1) * V7X_LANES)
        for r0 in range(0, tl, CONV_ROWS):
            acc = jnp.zeros((CONV_ROWS, V7X_LANES), F32)
            for j in range(CONV_TAPS):
                q, s = divmod(j, V7X_SUBLANES)
                rows = slice(r0 + q * V7X_SUBLANES, r0 + q * V7X_SUBLANES + CONV_ROWS)
                acc = acc + cw_ref[j:j + 1, lanes] * vsh_ref[s, rows, lanes]
            y_ref[r0:r0 + CONV_ROWS, lanes] = acc + cb_ref[:, lanes]
    y = y_ref[...]
    mu = jnp.mean(y, axis=-1, keepdims=True)
    yc = y - mu
    var = jnp.mean(yc * yc, axis=-1, keepdims=True)
    yn = yc * lax.rsqrt(var + LN_EPS) * lng_ref[...] + lnb_ref[...]
    yc_ref[...] = (yn * jax.nn.sigmoid(yn)).astype(BF16)


def _mix_in(x, g, w_in, conv_w, conv_b, ln_g, ln_b, bsz, seq):
    t = x.shape[0]
    nl = seq // ROW_TILE
    row = lambda w: pl.BlockSpec((ROW_TILE, w), lambda b, l: (b * nl + l, 0))
    return pl.pallas_call(
        _mix_in_body, grid=(bsz, nl),
        in_specs=[row(D_MODEL), _resident((1, D_MODEL)), _resident((D_MODEL, IN_WIDTH)),
                  _resident((CONV_TAPS, CONV_CH)), _resident((1, CONV_CH)), _resident((1, CONV_CH)),
                  _resident((1, CONV_CH))],
        out_specs=[row(CONV_CH), row(SSM_WIDTH), _chunk_layout_spec()],
        out_shape=[jax.ShapeDtypeStruct((t, CONV_CH), BF16), jax.ShapeDtypeStruct((t, SSM_WIDTH), F32),
                   jax.ShapeDtypeStruct((S5_BLOCKS, seq // S5_CHUNK, bsz * S5_WIDE), BF16)],
        scratch_shapes=[pltpu.VMEM((ROW_TILE + CONV_HALO, CONV_CH), F32),
                        pltpu.VMEM((V7X_SUBLANES, ROW_TILE + CONV_HALO - V7X_SUBLANES, CONV_CH), F32),
                        pltpu.VMEM((ROW_TILE, CONV_CH), F32),
                        pltpu.VMEM((S5_BLOCKS, ROW_TILE, V7X_LANES), F32)],
        compiler_params=_params(2, 48 * 1024 * 1024), name="mix_in")(
            x, g.reshape(1, D_MODEL), w_in.astype(BF16), conv_w, conv_b.reshape(1, CONV_CH),
            ln_g.reshape(1, CONV_CH), ln_b.reshape(1, CONV_CH))


def _s5_tables(a_re, a_im, b_re, b_im, c_re, c_im, log_dt):
    dt = jnp.exp(log_dt.astype(F32))[:, None]
    a_re, a_im = a_re.astype(F32), a_im.astype(F32)
    steps = jnp.arange(S5_CHUNK + 1, dtype=F32)[:, None, None]
    mag = jnp.exp(steps * (dt * a_re)[None])
    ang = steps * (dt * a_im)[None]
    pw_re, pw_im = mag * jnp.cos(ang), mag * jnp.sin(ang)
    den = a_re * a_re + a_im * a_im
    nr, ni = pw_re[1] - 1.0, pw_im[1]
    q_re = (nr * a_re + ni * a_im) / den
    q_im = (ni * a_re - nr * a_im) / den
    bt_re, bt_im = b_re.astype(F32).transpose(0, 2, 1), b_im.astype(F32).transpose(0, 2, 1)
    ct_re, ct_im = c_re.astype(F32).transpose(0, 2, 1), c_im.astype(F32).transpose(0, 2, 1)
    bb_re = q_re[:, None, :] * bt_re - q_im[:, None, :] * bt_im
    bb_im = q_re[:, None, :] * bt_im + q_im[:, None, :] * bt_re
    lb_re = pw_re[:-1, :, None, :] * bb_re[None] - pw_im[:-1, :, None, :] * bb_im[None]
    lb_im = pw_re[:-1, :, None, :] * bb_im[None] + pw_im[:-1, :, None, :] * bb_re[None]
    hi = lax.Precision.HIGHEST
    k = (jnp.einsum("ghp,jgip->jgih", c_re, lb_re, precision=hi)
         - jnp.einsum("ghp,jgip->jgih", c_im, lb_im, precision=hi))
    cl_re = ct_re[None] * pw_re[1:, :, :, None] - ct_im[None] * pw_im[1:, :, :, None]
    cl_im = ct_re[None] * pw_im[1:, :, :, None] + ct_im[None] * pw_re[1:, :, :, None]

    def per_block(m):
        return m.reshape(S5_CHUNK, S5_BLOCKS, S5_GPB * m.shape[2], m.shape[3]).transpose(1, 0, 2, 3)

    kc = per_block(k).astype(BF16)
    lbc = jnp.stack([per_block(lb_re[::-1]), per_block(lb_im[::-1])], axis=2).astype(BF16)
    clc = jnp.stack([per_block(cl_re), per_block(-cl_im)], axis=2).astype(BF16)
    a_chunk = jnp.stack([pw_re[S5_CHUNK].reshape(S5_BLOCKS, -1), pw_im[S5_CHUNK].reshape(S5_BLOCKS, -1)], axis=1)
    return _s5_expand(kc, lbc, clc) + (a_chunk,)


def _s5_expand_body(kc_ref, lbc_ref, clc_ref, toep_ref, in_ref, out_ref):
    half = S5_GPB * SSM_STATE
    iota = lambda shape, d: lax.broadcasted_iota(jnp.int32, shape, d)
    one_hot = lambda c: jnp.where(c, 1.0, 0.0).astype(BF16)
    shift_g, shift_p = SSM_GROUP.bit_length() - 1, SSM_STATE.bit_length() - 1
    rep_h = one_hot((iota((SSM_GROUP, V7X_LANES), 1) & (SSM_GROUP - 1)) == iota((SSM_GROUP, V7X_LANES), 0))
    rep_p = one_hot((iota((SSM_STATE, half), 1) & (SSM_STATE - 1)) == iota((SSM_STATE, half), 0))
    same_hh = (iota((V7X_LANES, V7X_LANES), 0) >> shift_g) == (iota((V7X_LANES, V7X_LANES), 1) >> shift_g)
    same_hp = (iota((V7X_LANES, half), 0) >> shift_g) == (iota((V7X_LANES, half), 1) >> shift_p)
    same_ph = (iota((half, V7X_LANES), 0) >> shift_p) == (iota((half, V7X_LANES), 1) >> shift_g)
    for rr in range(S5_EXPAND_STEPS):
        r = pl.program_id(1) * S5_EXPAND_STEPS + rr
        rows = slice(rr * V7X_LANES, (rr + 1) * V7X_LANES)
        for s in range(S5_CHUNK):
            blk = _dot(kc_ref[jnp.maximum(s - r, 0)], rep_h)
            blk = jnp.where(same_hh, blk, 0.0) * jnp.where(s >= r, 1.0, 0.0)
            toep_ref[rows, s * V7X_LANES:(s + 1) * V7X_LANES] = blk.astype(BF16)
        for ri in range(2):
            in_ref[rows, ri * half:(ri + 1) * half] = (
                jnp.where(same_hp, _dot(lbc_ref[rr, ri], rep_p), 0.0).astype(BF16))
            out_ref[ri * half:(ri + 1) * half, rows] = (
                jnp.where(same_ph, _dot(clc_ref[rr, ri], rep_h), 0.0).astype(BF16))


def _s5_expand(kc, lbc, clc):
    half = S5_GPB * SSM_STATE
    return pl.pallas_call(
        _s5_expand_body, grid=(S5_BLOCKS, S5_CHUNK // S5_EXPAND_STEPS),
        in_specs=[pl.BlockSpec((None, S5_CHUNK, V7X_LANES, SSM_GROUP), lambda b, r: (b, 0, 0, 0)),
                  pl.BlockSpec((None, S5_EXPAND_STEPS, 2, V7X_LANES, SSM_STATE), lambda b, r: (b, r, 0, 0, 0)),
                  pl.BlockSpec((None, S5_EXPAND_STEPS, 2, half, SSM_GROUP), lambda b, r: (b, r, 0, 0, 0))],
        out_specs=[pl.BlockSpec((None, S5_EXPAND_STEPS * V7X_LANES, S5_WIDE), lambda b, r: (b, r, 0)),
                   pl.BlockSpec((None, S5_EXPAND_STEPS * V7X_LANES, 2 * half), lambda b, r: (b, r, 0)),
                   pl.BlockSpec((None, 2 * half, S5_EXPAND_STEPS * V7X_LANES), lambda b, r: (b, 0, r))],
        out_shape=[jax.ShapeDtypeStruct((S5_BLOCKS, S5_WIDE, S5_WIDE), BF16),
                   jax.ShapeDtypeStruct((S5_BLOCKS, S5_WIDE, 2 * half), BF16),
                   jax.ShapeDtypeStruct((S5_BLOCKS, 2 * half, S5_WIDE), BF16)],
        compiler_params=_params(2, 32 * 1024 * 1024), name="s5_expand")(kc, lbc, clc)


def _s5_body(u_ref, t_ref, b_ref, c_ref, a_ref, y_ref, s_ref, x_ref, xr_ref, xi_ref, *, nb):
    half = S5_GPB * SSM_STATE

    @pl.when(pl.program_id(1) == 0)
    def _():
        xr_ref[...] = jnp.zeros((nb, half), F32)
        xi_ref[...] = jnp.zeros((nb, half), F32)

    s_ref[...] = _dot(u_ref[...], b_ref[...])
    ar = a_ref[0:1, :]
    ai = a_ref[1:2, :]

    def step(c, carry):
        xr, xi = carry
        rows = pl.ds(pl.multiple_of(c * nb, nb), nb)
        x_ref[rows, 0:half] = xr.astype(BF16)
        x_ref[rows, half:] = xi.astype(BF16)
        return (ar * xr - ai * xi + s_ref[rows, 0:half], ar * xi + ai * xr + s_ref[rows, half:])

    xr, xi = lax.fori_loop(0, u_ref.shape[0] // nb, step, (xr_ref[...], xi_ref[...]))
    xr_ref[...] = xr
    xi_ref[...] = xi
    for nt in range(S5_WIDE // S5_COLS):
        cols = slice(nt * S5_COLS, (nt + 1) * S5_COLS)
        k = (nt + 1) * S5_COLS
        y_ref[:, cols] = _dot(u_ref[:, 0:k], t_ref[0:k, cols]) + _dot(x_ref[...], c_ref[:, cols])


def _s5(u8, tables, bsz, seq):
    nchunk = seq // S5_CHUNK
    rows = S5_ROWS_CHUNKS * bsz
    half = S5_GPB * SSM_STATE
    tab = lambda a, b: pl.BlockSpec((None, a, b), lambda g, c: (g, 0, 0), pipeline_mode=pl.Buffered(1))
    act = pl.BlockSpec((None, rows, S5_WIDE), lambda g, c: (g, c, 0))
    return pl.pallas_call(
        functools.partial(_s5_body, nb=bsz), grid=(S5_BLOCKS, nchunk // S5_ROWS_CHUNKS),
        in_specs=[act, tab(S5_WIDE, S5_WIDE), tab(S5_WIDE, 2 * half), tab(2 * half, S5_WIDE), tab(2, half)],
        out_specs=act, out_shape=jax.ShapeDtypeStruct((S5_BLOCKS, nchunk * bsz, S5_WIDE), F32),
        scratch_shapes=[pltpu.VMEM((rows, 2 * half), F32), pltpu.VMEM((rows, 2 * half), BF16),
                        pltpu.VMEM((bsz, half), F32), pltpu.VMEM((bsz, half), F32)],
        compiler_params=_params(2, 48 * 1024 * 1024), name="s5")(u8, *tables)


def _chunk_layout_spec():
    return pl.BlockSpec((S5_BLOCKS, ROW_TILE // S5_CHUNK, S5_WIDE), lambda b, l: (0, l, b))


def _mix_out_body(x_ref, yc_ref, y8_ref, u_ref, d_ref, gw_ref, gb_ref, wc_ref, ws_ref, *rest):
    ffn_refs, (o_ref, ys_ref) = rest[:-2], rest[-2:]
    tl = x_ref.shape[0]
    for gb in range(S5_BLOCKS):
        for s in range(S5_CHUNK):
            ys_ref[gb, pl.ds(s, tl // S5_CHUNK, stride=S5_CHUNK), :] = (
                y8_ref[gb, :, s * V7X_LANES:(s + 1) * V7X_LANES])
    ys = jnp.concatenate([ys_ref[gb] for gb in range(S5_BLOCKS)], axis=1)
    y = ys + d_ref[...] * u_ref[...]
    y = 0.5 * y * (1.0 + jnp.tanh(math.sqrt(2.0 / math.pi) * (y + 0.044715 * (y * y * y))))
    gate = jax.nn.sigmoid(_dot(y.astype(BF16), gw_ref[...]) + gb_ref[...])
    y = (y * gate).astype(BF16)
    x = x_ref[...] + _dot(yc_ref[...], wc_ref[...]) + _dot(y, ws_ref[...])
    o_ref[...] = _half_ffn(x, *ffn_refs) if ffn_refs else x


def _mix_out(x, y_conv, y_state, u, d, glu_w, glu_b, w_out, bsz, seq, ffn=None):
    t = x.shape[0]
    nl = seq // ROW_TILE
    row = lambda w: pl.BlockSpec((ROW_TILE, w), lambda b, l: (b * nl + l, 0))
    in_specs = [row(D_MODEL), row(CONV_CH), _chunk_layout_spec(), row(SSM_WIDTH), _resident((1, SSM_WIDTH)),
                _resident((SSM_WIDTH, SSM_WIDTH)), _resident((1, SSM_WIDTH)),
                _resident((CONV_CH, D_MODEL)), _resident((SSM_WIDTH, D_MODEL))]
    args = [x, y_conv, y_state, u, d.reshape(1, SSM_WIDTH), glu_w.astype(BF16), glu_b.reshape(1, SSM_WIDTH),
            w_out[:CONV_CH].astype(BF16), w_out[CONV_CH:].astype(BF16)]
    if ffn is not None:
        g, w1, w3, w2, which = ffn
        in_specs += _ffn_weight_specs(which)
        args += [g.reshape(1, D_MODEL), w1, w3, w2]
    return pl.pallas_call(
        _mix_out_body, grid=(bsz, nl), in_specs=in_specs,
        out_specs=row(D_MODEL), out_shape=jax.ShapeDtypeStruct((t, D_MODEL), F32),
        scratch_shapes=[pltpu.VMEM((S5_BLOCKS, ROW_TILE, V7X_LANES), F32)],
        compiler_params=_params(2, 52 * 1024 * 1024), name="mix_out")(*args)


def _qkv_body(x_ref, g_ref, w_ref, qkv_ref, km_ref):
    h = _rms(x_ref[...], g_ref[...]).astype(BF16)
    r = _dot(h, w_ref[...])
    qkv_ref[...] = r.astype(BF16)
    for blk in range(x_ref.shape[0] // MOBA_BLOCK):
        k = r[blk * MOBA_BLOCK:(blk + 1) * MOBA_BLOCK, D_MODEL:2 * D_MODEL]
        km_ref[blk] = jnp.mean(k, axis=0, keepdims=True)


def _qkv(x, g, w_qkv):
    t = x.shape[0]
    nb = ROW_TILE // MOBA_BLOCK
    return pl.pallas_call(
        _qkv_body, grid=(t // ROW_TILE,),
        in_specs=[pl.BlockSpec((ROW_TILE, D_MODEL), lambda i: (i, 0)), _resident((1, D_MODEL)),
                  _resident((D_MODEL, 3 * D_MODEL))],
        out_specs=[pl.BlockSpec((ROW_TILE, 3 * D_MODEL), lambda i: (i, 0)),
                   pl.BlockSpec((nb, 1, D_MODEL), lambda i: (i, 0, 0))],
        out_shape=[jax.ShapeDtypeStruct((t, 3 * D_MODEL), BF16),
                   jax.ShapeDtypeStruct((t // MOBA_BLOCK, 1, D_MODEL), F32)],
        compiler_params=_params(1, 40 * 1024 * 1024), name="qkv")(
            x, g.reshape(1, D_MODEL), w_qkv.astype(BF16))


def _moba_body(q_ref, k_ref, v_ref, km_ref, o_ref, kaug_ref, vaug_ref):
    seq = q_ref.shape[0]
    nblk = seq // MOBA_BLOCK
    lane = lax.broadcasted_iota(jnp.int32, (seq, HEAD_DIM), 1)
    blk_of_row = lax.broadcasted_iota(jnp.int32, (seq, HEAD_DIM), 0) // MOBA_BLOCK
    block_one_hot = jnp.where(lane == blk_of_row, 1.0, 0.0).astype(BF16)
    for hh in range(MOBA_HEADS_PER_STEP):
        head = slice(hh * HEAD_DIM, (hh + 1) * HEAD_DIM)
        kaug_ref[hh, :, 0:HEAD_DIM] = k_ref[:, head]
        kaug_ref[hh, :, HEAD_DIM:] = block_one_hot
        vaug_ref[hh, :, 0:HEAD_DIM] = v_ref[:, head]
        vaug_ref[hh, :, HEAD_DIM:] = jnp.ones((seq, HEAD_DIM), BF16)
    blk_id = lax.broadcasted_iota(jnp.int32, (nblk, MOBA_BLOCK), 0)
    qi = lax.broadcasted_iota(jnp.int32, (MOBA_BLOCK, MOBA_BLOCK), 0)
    ki = lax.broadcasted_iota(jnp.int32, (MOBA_BLOCK, MOBA_BLOCK), 1)
    for i, hh in [(i, hh) for i in range(nblk) for hh in range(MOBA_HEADS_PER_STEP)]:
        head = slice(hh * HEAD_DIM, (hh + 1) * HEAD_DIM)
        kaug, vaug = kaug_ref.at[hh], vaug_ref.at[hh]
        own = slice(i * MOBA_BLOCK, (i + 1) * MOBA_BLOCK)
        q = q_ref[own, head]
        s_own = jnp.where(ki <= qi, _dot_nt(q, k_ref[own, head]), MASK_VALUE)
        m = jnp.max(s_own, axis=1, keepdims=True)
        if i > 0:
            past = blk_id < i
            if i > MOBA_TOPK:
                gate = _dot_nt(km_ref[:, head].astype(BF16), q)
                rank = jnp.zeros((nblk, MOBA_BLOCK), jnp.int32)
                for j in range(i):
                    gj = gate[j:j + 1, :]
                    ahead = jnp.where(gj > gate, 1, jnp.where(gj == gate, jnp.where(blk_id > j, 1, 0), 0))
                    rank = rank + ahead
                chosen = jnp.where(past, rank, MOBA_TOPK) < MOBA_TOPK
            else:
                chosen = past
            bias = jnp.where(chosen, 0.0, MASK_VALUE)
            bias = jnp.concatenate([bias, jnp.zeros((HEAD_DIM - nblk, MOBA_BLOCK), F32)], axis=0)
            q_aug = jnp.concatenate([q, bias.T.astype(BF16)], axis=1)
            s_past = _dot_nt(q_aug, kaug[0:i * MOBA_BLOCK, :])
            m = jnp.maximum(m, jnp.max(s_past, axis=1, keepdims=True))
            p_past = jnp.exp2((s_past - m) * SOFTMAX_EXP2_SCALE).astype(BF16)
        p_own = jnp.exp2((s_own - m) * SOFTMAX_EXP2_SCALE).astype(BF16)
        o = _dot(p_own, vaug[own, :])
        if i > 0:
            o = o + _dot(p_past, vaug[0:i * MOBA_BLOCK, :])
        o_ref[own, head] = (o[:, :HEAD_DIM] / o[:, HEAD_DIM:]).astype(BF16)


def _moba(qkv, kmean, bsz, seq):
    nblk = seq // MOBA_BLOCK
    width = MOBA_HEADS_PER_STEP * HEAD_DIM
    steps = N_HEADS // MOBA_HEADS_PER_STEP
    col = lambda off: pl.BlockSpec((seq, width), lambda b, h: (b, off + h))
    aug = pltpu.VMEM((MOBA_HEADS_PER_STEP, seq, 2 * HEAD_DIM), BF16)
    return pl.pallas_call(
        _moba_body, grid=(bsz, steps),
        in_specs=[col(0), col(steps), col(2 * steps), pl.BlockSpec((nblk, width), lambda b, h: (b, h))],
        out_specs=col(0), out_shape=jax.ShapeDtypeStruct((bsz * seq, D_MODEL), BF16),
        scratch_shapes=[aug, aug],
        compiler_params=_params(2, 40 * 1024 * 1024), name="moba")(qkv, qkv, qkv, kmean)


def kernel(x, ffn_norm, ffn_w1, ffn_w3, ffn_w2, mix_norm, ab_w_in, conv_w, conv_b, conv_ln_g, conv_ln_b, ssm_a_re, ssm_a_im, ssm_b_re, ssm_b_im, ssm_c_re, ssm_c_im, ssm_d, ssm_log_dt, ssm_glu_w, ssm_glu_b, ab_w_out, attn_w_qkv, attn_w_o, final_norm):
    bsz, seq, d = x.shape
    depth = ffn_norm.shape[0]
    assert d == D_MODEL and seq % ROW_TILE == 0 and ROW_TILE % MOBA_BLOCK == 0
    assert bsz % (2 * V7X_SUBLANES) == 0, "the S5 boundary scan walks whole packed bf16 tiles of batch rows"
    x = x.reshape(bsz * seq, d)
    w1, w3, w2 = ffn_w1.astype(BF16), ffn_w3.astype(BF16), ffn_w2.astype(BF16)
    for l in range(depth):
        last = l == depth - 1
        x = _ffn(x, ffn_norm[l, 0], w1, w3, w2, (l, 0))
        if l % 2 == 0:
            e = l // 2
            y_conv, u, u8 = _mix_in(x, mix_norm[l], ab_w_in[e], conv_w[e], conv_b[e], conv_ln_g[e],
                                    conv_ln_b[e], bsz, seq)
            tables = _s5_tables(ssm_a_re[e], ssm_a_im[e], ssm_b_re[e], ssm_b_im[e], ssm_c_re[e],
                                ssm_c_im[e], ssm_log_dt[e])
            nchunk = seq // S5_CHUNK
            y8 = _s5(u8.reshape(S5_BLOCKS, nchunk * bsz, S5_WIDE), tables, bsz, seq)
            x = _mix_out(x, y_conv, y8.reshape(S5_BLOCKS, nchunk, bsz * S5_WIDE), u, ssm_d[e], ssm_glu_w[e],
                         ssm_glu_b[e], ab_w_out[e], bsz, seq,
                         ffn=None if last else (ffn_norm[l, 1], w1, w3, w2, (l, 1)))
            if not last:
                continue
            attn = None
        else:
            o = l // 2
            qkv, kmean = _qkv(x, mix_norm[l], attn_w_qkv[o])
            attn = (_moba(qkv, kmean.reshape(-1, D_MODEL), bsz, seq), attn_w_o[o])
        x = _ffn(x, ffn_norm[l, 1], w1, w3, w2, (l, 1), final_g=final_norm if last else None, attn=attn)
    return x.reshape(bsz, seq, d)
```

```python
import functools
import math

import jax
import jax.numpy as jnp
from jax import lax
from jax.experimental import pallas as pl
from jax.experimental.pallas import tpu as pltpu

F32 = jnp.float32
BF16 = jnp.bfloat16

D_MODEL = 1024
D_FF = 2816
RMS_EPS = 1e-6
LN_EPS = 1e-5
CONV_CH = 512
CONV_TAPS = 31
SSM_WIDTH = 512
SSM_GROUP = 16
SSM_GROUPS = SSM_WIDTH // SSM_GROUP
SSM_STATE = 64
IN_WIDTH = 2 * CONV_CH + SSM_WIDTH
N_HEADS = 8
HEAD_DIM = 128
MOBA_BLOCK = 256
MOBA_TOPK = 3

V7X_LANES = 128
V7X_SUBLANES = 8
V7X_VMEM_BYTES = 64 * 1024 * 1024

ROW_TILE = 512
FF_CHUNK = 256
CONV_HALO = 32
CONV_ROWS = 128
S5_CHUNK = 16
S5_BLOCKS = SSM_WIDTH // V7X_LANES
S5_GPB = V7X_LANES // SSM_GROUP
S5_WIDE = S5_CHUNK * V7X_LANES
S5_COLS = 256
S5_ROWS_CHUNKS = 32
S5_EXPAND_STEPS = 4
MOBA_HEADS_PER_STEP = 4
MASK_VALUE = -1e30
SOFTMAX_EXP2_SCALE = HEAD_DIM ** -0.5 * math.log2(math.e)


def _vmem_limit(n_bytes):
    return int(min(n_bytes, V7X_VMEM_BYTES - 8 * 1024 * 1024))


def _params(n_grid, vmem_bytes):
    return pltpu.CompilerParams(
        dimension_semantics=("arbitrary",) * n_grid, vmem_limit_bytes=_vmem_limit(vmem_bytes))


def _resident(shape):
    nd = len(shape)
    return pl.BlockSpec(shape, lambda *_: (0,) * nd, pipeline_mode=pl.Buffered(1))


def _rms(x, g):
    return x * lax.rsqrt(jnp.mean(x * x, axis=-1, keepdims=True) + RMS_EPS) * g


def _dot(a, b, **kw):
    return jnp.dot(a, b, preferred_element_type=F32, **kw)


def _dot_nt(a, b):
    return lax.dot_general(a, b, (((1,), (1,)), ((), ())), preferred_element_type=F32)


def _half_ffn(x, g_ref, w1_ref, w3_ref, w2_ref):
    h = _rms(x, g_ref[...]).astype(BF16)
    acc = None
    for c0 in range(0, D_FF, FF_CHUNK):
        a = _dot(h, w1_ref[:, c0:c0 + FF_CHUNK])
        b = _dot(h, w3_ref[:, c0:c0 + FF_CHUNK])
        act = (a * jax.nn.sigmoid(a) * b).astype(BF16)
        part = _dot(act, w2_ref[c0:c0 + FF_CHUNK, :])
        acc = part if acc is None else acc + part
    return x + 0.5 * acc


def _ffn_weight_specs(which):
    layer, half = which
    pick = lambda a, b: pl.BlockSpec((None, None, a, b), lambda *_: (layer, half, 0, 0),
                                     pipeline_mode=pl.Buffered(1))
    return [_resident((1, D_MODEL)), pick(D_MODEL, D_FF), pick(D_MODEL, D_FF), pick(D_FF, D_MODEL)]


def _ffn_body(*refs, final, attn, qkv):
    n_out = 3 if qkv else 1
    x_ref, g_ref, w1_ref, w3_ref, w2_ref = refs[:5]
    o_ref = refs[-n_out]
    extra = list(refs[5:-n_out])
    x = x_ref[...]
    if attn:
        att_ref, wo_ref = extra[:2]
        extra = extra[2:]
        x = x + _dot(att_ref[...], wo_ref[...])
    y = _half_ffn(x, g_ref, w1_ref, w3_ref, w2_ref)
    if final:
        y = _rms(y, extra[0][...])
    o_ref[...] = y
    if qkv:
        gm_ref, wqkv_ref = extra
        qkv_ref, km_ref = refs[-2:]
        r = _dot(_rms(y, gm_ref[...]).astype(BF16), wqkv_ref[...])
        qkv_ref[...] = r.astype(BF16)
        for blk in range(x_ref.shape[0] // MOBA_BLOCK):
            k = r[blk * MOBA_BLOCK:(blk + 1) * MOBA_BLOCK, D_MODEL:2 * D_MODEL]
            km_ref[blk] = jnp.mean(k, axis=0, keepdims=True)


def _ffn(x, g, w1, w3, w2, which, final_g=None, attn=None, qkv=None):
    assert final_g is None or qkv is None
    t = x.shape[0]
    row = pl.BlockSpec((ROW_TILE, D_MODEL), lambda i: (i, 0))
    in_specs = [row] + _ffn_weight_specs(which)
    args = [x, g.reshape(1, D_MODEL), w1, w3, w2]
    out_specs, out_shape = [row], [jax.ShapeDtypeStruct((t, D_MODEL), F32)]
    if attn is not None:
        in_specs += [row, _resident((D_MODEL, D_MODEL))]
        args += [attn[0], attn[1].astype(BF16)]
    if final_g is not None:
        in_specs.append(_resident((1, D_MODEL)))
        args.append(final_g.reshape(1, D_MODEL))
    if qkv is not None:
        in_specs += [_resident((1, D_MODEL)), _resident((D_MODEL, 3 * D_MODEL))]
        args += [qkv[0].reshape(1, D_MODEL), qkv[1].astype(BF16)]
        out_specs += [pl.BlockSpec((ROW_TILE, 3 * D_MODEL), lambda i: (i, 0)),
                      pl.BlockSpec((ROW_TILE // MOBA_BLOCK, 1, D_MODEL), lambda i: (i, 0, 0))]
        out_shape += [jax.ShapeDtypeStruct((t, 3 * D_MODEL), BF16),
                      jax.ShapeDtypeStruct((t // MOBA_BLOCK, 1, D_MODEL), F32)]
    out = pl.pallas_call(
        functools.partial(_ffn_body, final=final_g is not None, attn=attn is not None, qkv=qkv is not None),
        grid=(t // ROW_TILE,), in_specs=in_specs, out_specs=out_specs, out_shape=out_shape,
        compiler_params=_params(1, 54 * 1024 * 1024), name="ffn")(*args)
    return out if qkv is not None else out[0]


def _mix_in_body(x_ref, g_ref, win_ref, cw_ref, cb_ref, lng_ref, lnb_ref, yc_ref, u_ref, u8_ref,
                 vpad_ref, vsh_ref, y_ref, ub_ref):
    tl = x_ref.shape[0]

    @pl.when(pl.program_id(1) == 0)
    def _():
        vpad_ref[0:CONV_HALO, :] = jnp.zeros((CONV_HALO, CONV_CH), F32)

    h = _rms(x_ref[...], g_ref[...]).astype(BF16)
    p = _dot(h, win_ref[...])
    u_ref[...] = p[:, 2 * CONV_CH:]
    vpad_ref[CONV_HALO:CONV_HALO + tl, :] = p[:, :CONV_CH] * jax.nn.sigmoid(p[:, CONV_CH:2 * CONV_CH])
    for gb in range(S5_BLOCKS):
        ub_ref[gb] = p[:, 2 * CONV_CH + gb * V7X_LANES:2 * CONV_CH + (gb + 1) * V7X_LANES]
        for s in range(S5_CHUNK):
            u8_ref[gb, :, s * V7X_LANES:(s + 1) * V7X_LANES] = ub_ref[
                gb, pl.ds(s, tl // S5_CHUNK, stride=S5_CHUNK), :].astype(BF16)
    first = CONV_HALO - (CONV_TAPS - 1)
    for s in range(V7X_SUBLANES):
        span = tl + (CONV_TAPS - 1 - s) // V7X_SUBLANES * V7X_SUBLANES
        vsh_ref[s, 0:span, :] = vpad_ref[first + s:first + s + span, :]
    vpad_ref[0:CONV_HALO, :] = vpad_ref[tl:tl + CONV_HALO, :]
    for lb in range(CONV_CH // V7X_LANES):
        lanes = slice(lb * V7X_LANES, (lb + 1) * V7X_LANES)
        for r0 in range(0, tl, CONV_ROWS):
            acc = jnp.zeros((CONV_ROWS, V7X_LANES), F32)
            for j in range(CONV_TAPS):
                q, s = divmod(j, V7X_SUBLANES)
                rows = slice(r0 + q * V7X_SUBLANES, r0 + q * V7X_SUBLANES + CONV_ROWS)
                acc = acc + cw_ref[j:j + 1, lanes] * vsh_ref[s, rows, lanes]
            y_ref[r0:r0 + CONV_ROWS, lanes] = acc + cb_ref[:, lanes]
    y = y_ref[...]
    mu = jnp.mean(y, axis=-1, keepdims=True)
    yc = y - mu
    var = jnp.mean(yc * yc, axis=-1, keepdims=True)
    yn = yc * lax.rsqrt(var + LN_EPS) * lng_ref[...] + lnb_ref[...]
    yc_ref[...] = (yn * jax.nn.sigmoid(yn)).astype(BF16)


def _mix_in(x, g, w_in, conv_w, conv_b, ln_g, ln_b, bsz, seq):
    t = x.shape[0]
    nl = seq // ROW_TILE
    row = lambda w: pl.BlockSpec((ROW_TILE, w), lambda b, l: (b * nl + l, 0))
    return pl.pallas_call(
        _mix_in_body, grid=(bsz, nl),
        in_specs=[row(D_MODEL), _resident((1, D_MODEL)), _resident((D_MODEL, IN_WIDTH)),
                  _resident((CONV_TAPS, CONV_CH)), _resident((1, CONV_CH)), _resident((1, CONV_CH)),
                  _resident((1, CONV_CH))],
        out_specs=[row(CONV_CH), row(SSM_WIDTH), _chunk_layout_spec()],
        out_shape=[jax.ShapeDtypeStruct((t, CONV_CH), BF16), jax.ShapeDtypeStruct((t, SSM_WIDTH), F32),
                   jax.ShapeDtypeStruct((S5_BLOCKS, seq // S5_CHUNK, bsz * S5_WIDE), BF16)],
        scratch_shapes=[pltpu.VMEM((ROW_TILE + CONV_HALO, CONV_CH), F32),
                        pltpu.VMEM((V7X_SUBLANES, ROW_TILE + CONV_HALO - V7X_SUBLANES, CONV_CH), F32),
                        pltpu.VMEM((ROW_TILE, CONV_CH), F32),
                        pltpu.VMEM((S5_BLOCKS, ROW_TILE, V7X_LANES), F32)],
        compiler_params=_params(2, 48 * 1024 * 1024), name="mix_in")(
            x, g.reshape(1, D_MODEL), w_in.astype(BF16), conv_w, conv_b.reshape(1, CONV_CH),
            ln_g.reshape(1, CONV_CH), ln_b.reshape(1, CONV_CH))


def _s5_tables(a_re, a_im, b_re, b_im, c_re, c_im, log_dt):
    dt = jnp.exp(log_dt.astype(F32))[:, None]
    a_re, a_im = a_re.astype(F32), a_im.astype(F32)
    steps = jnp.arange(S5_CHUNK + 1, dtype=F32)[:, None, None]
    mag = jnp.exp(steps * (dt * a_re)[None])
    ang = steps * (dt * a_im)[None]
    pw_re, pw_im = mag * jnp.cos(ang), mag * jnp.sin(ang)
    den = a_re * a_re + a_im * a_im
    nr, ni = pw_re[1] - 1.0, pw_im[1]
    q_re = (nr * a_re + ni * a_im) / den
    q_im = (ni * a_re - nr * a_im) / den
    bt_re, bt_im = b_re.astype(F32).transpose(0, 2, 1), b_im.astype(F32).transpose(0, 2, 1)
    ct_re, ct_im = c_re.astype(F32).transpose(0, 2, 1), c_im.astype(F32).transpose(0, 2, 1)
    bb_re = q_re[:, None, :] * bt_re - q_im[:, None, :] * bt_im
    bb_im = q_re[:, None, :] * bt_im + q_im[:, None, :] * bt_re
    lb_re = pw_re[:-1, :, None, :] * bb_re[None] - pw_im[:-1, :, None, :] * bb_im[None]
    lb_im = pw_re[:-1, :, None, :] * bb_im[None] + pw_im[:-1, :, None, :] * bb_re[None]
    hi = lax.Precision.HIGHEST
    k = (jnp.einsum("ghp,jgip->jgih", c_re, lb_re, precision=hi)
         - jnp.einsum("ghp,jgip->jgih", c_im, lb_im, precision=hi))
    cl_re = ct_re[None] * pw_re[1:, :, :, None] - ct_im[None] * pw_im[1:, :, :, None]
    cl_im = ct_re[None] * pw_im[1:, :, :, None] + ct_im[None] * pw_re[1:, :, :, None]

    def per_block(m):
        return m.reshape(S5_CHUNK, S5_BLOCKS, S5_GPB * m.shape[2], m.shape[3]).transpose(1, 0, 2, 3)

    kc = per_block(k).astype(BF16)
    lbc = jnp.stack([per_block(lb_re[::-1]), per_block(lb_im[::-1])], axis=2).astype(BF16)
    clc = jnp.stack([per_block(cl_re), per_block(-cl_im)], axis=2).astype(BF16)
    a_chunk = jnp.stack([pw_re[S5_CHUNK].reshape(S5_BLOCKS, -1), pw_im[S5_CHUNK].reshape(S5_BLOCKS, -1)], axis=1)
    return _s5_expand(kc, lbc, clc) + (a_chunk,)


def _s5_expand_body(kc_ref, lbc_ref, clc_ref, toep_ref, in_ref, out_ref):
    half = S5_GPB * SSM_STATE
    iota = lambda shape, d: lax.broadcasted_iota(jnp.int32, shape, d)
    one_hot = lambda c: jnp.where(c, 1.0, 0.0).astype(BF16)
    shift_g, shift_p = SSM_GROUP.bit_length() - 1, SSM_STATE.bit_length() - 1
    rep_h = one_hot((iota((SSM_GROUP, V7X_LANES), 1) & (SSM_GROUP - 1)) == iota((SSM_GROUP, V7X_LANES), 0))
    rep_p = one_hot((iota((SSM_STATE, half), 1) & (SSM_STATE - 1)) == iota((SSM_STATE, half), 0))
    same_hh = (iota((V7X_LANES, V7X_LANES), 0) >> shift_g) == (iota((V7X_LANES, V7X_LANES), 1) >> shift_g)
    same_hp = (iota((V7X_LANES, half), 0) >> shift_g) == (iota((V7X_LANES, half), 1) >> shift_p)
    same_ph = (iota((half, V7X_LANES), 0) >> shift_p) == (iota((half, V7X_LANES), 1) >> shift_g)
    for rr in range(S5_EXPAND_STEPS):
        r = pl.program_id(1) * S5_EXPAND_STEPS + rr
        rows = slice(rr * V7X_LANES, (rr + 1) * V7X_LANES)
        for s in range(S5_CHUNK):
            blk = _dot(kc_ref[jnp.maximum(s - r, 0)], rep_h)
            blk = jnp.where(same_hh, blk, 0.0) * jnp.where(s >= r, 1.0, 0.0)
            toep_ref[rows, s * V7X_LANES:(s + 1) * V7X_LANES] = blk.astype(BF16)
        for ri in range(2):
            in_ref[rows, ri * half:(ri + 1) * half] = (
                jnp.where(same_hp, _dot(lbc_ref[rr, ri], rep_p), 0.0).astype(BF16))
            out_ref[ri * half:(ri + 1) * half, rows] = (
                jnp.where(same_ph, _dot(clc_ref[rr, ri], rep_h), 0.0).astype(BF16))


def _s5_expand(kc, lbc, clc):
    half = S5_GPB * SSM_STATE
    return pl.pallas_call(
        _s5_expand_body, grid=(S5_BLOCKS, S5_CHUNK // S5_EXPAND_STEPS),
        in_specs=[pl.BlockSpec((None, S5_CHUNK, V7X_LANES, SSM_GROUP), lambda b, r: (b, 0, 0, 0)),
                  pl.BlockSpec((None, S5_EXPAND_STEPS, 2, V7X_LANES, SSM_STATE), lambda b, r: (b, r, 0, 0, 0)),
                  pl.BlockSpec((None, S5_EXPAND_STEPS, 2, half, SSM_GROUP), lambda b, r: (b, r, 0, 0, 0))],
        out_specs=[pl.BlockSpec((None, S5_EXPAND_STEPS * V7X_LANES, S5_WIDE), lambda b, r: (b, r, 0)),
                   pl.BlockSpec((None, S5_EXPAND_STEPS * V7X_LANES, 2 * half), lambda b, r: (b, r, 0)),
                   pl.BlockSpec((None, 2 * half, S5_EXPAND_STEPS * V7X_LANES), lambda b, r: (b, 0, r))],
        out_shape=[jax.ShapeDtypeStruct((S5_BLOCKS, S5_WIDE, S5_WIDE), BF16),
                   jax.ShapeDtypeStruct((S5_BLOCKS, S5_WIDE, 2 * half), BF16),
                   jax.ShapeDtypeStruct((S5_BLOCKS, 2 * half, S5_WIDE), BF16)],
        compiler_params=_params(2, 32 * 1024 * 1024), name="s5_expand")(kc, lbc, clc)


def _s5_body(u_ref, t_ref, b_ref, c_ref, a_ref, y_ref, s_ref, x_ref, xr_ref, xi_ref, *, nb):
    half = S5_GPB * SSM_STATE

    @pl.when(pl.program_id(1) == 0)
    def _():
        xr_ref[...] = jnp.zeros((nb, half), F32)
        xi_ref[...] = jnp.zeros((nb, half), F32)

    s_ref[...] = _dot(u_ref[...], b_ref[...])
    ar = a_ref[0:1, :]
    ai = a_ref[1:2, :]

    def step(c, carry):
        xr, xi = carry
        rows = pl.ds(pl.multiple_of(c * nb, nb), nb)
        x_ref[rows, 0:half] = xr.astype(BF16)
        x_ref[rows, half:] = xi.astype(BF16)
        return (ar * xr - ai * xi + s_ref[rows, 0:half], ar * xi + ai * xr + s_ref[rows, half:])

    xr, xi = lax.fori_loop(0, u_ref.shape[0] // nb, step, (xr_ref[...], xi_ref[...]))
    xr_ref[...] = xr
    xi_ref[...] = xi
    for nt in range(S5_WIDE // S5_COLS):
        cols = slice(nt * S5_COLS, (nt + 1) * S5_COLS)
        k = (nt + 1) * S5_COLS
        y_ref[:, cols] = _dot(u_ref[:, 0:k], t_ref[0:k, cols]) + _dot(x_ref[...], c_ref[:, cols])


def _s5(u8, tables, bsz, seq):
    nchunk = seq // S5_CHUNK
    rows = S5_ROWS_CHUNKS * bsz
    half = S5_GPB * SSM_STATE
    tab = lambda a, b: pl.BlockSpec((None, a, b), lambda g, c: (g, 0, 0), pipeline_mode=pl.Buffered(1))
    act = pl.BlockSpec((None, rows, S5_WIDE), lambda g, c: (g, c, 0))
    return pl.pallas_call(
        functools.partial(_s5_body, nb=bsz), grid=(S5_BLOCKS, nchunk // S5_ROWS_CHUNKS),
        in_specs=[act, tab(S5_WIDE, S5_WIDE), tab(S5_WIDE, 2 * half), tab(2 * half, S5_WIDE), tab(2, half)],
        out_specs=act, out_shape=jax.ShapeDtypeStruct((S5_BLOCKS, nchunk * bsz, S5_WIDE), F32),
        scratch_shapes=[pltpu.VMEM((rows, 2 * half), F32), pltpu.VMEM((rows, 2 * half), BF16),
                        pltpu.VMEM((bsz, half), F32), pltpu.VMEM((bsz, half), F32)],
        compiler_params=_params(2, 48 * 1024 * 1024), name="s5")(u8, *tables)


def _chunk_layout_spec():
    return pl.BlockSpec((S5_BLOCKS, ROW_TILE // S5_CHUNK, S5_WIDE), lambda b, l: (0, l, b))


def _mix_out_body(x_ref, yc_ref, y8_ref, u_ref, d_ref, gw_ref, gb_ref, wc_ref, ws_ref, *rest):
    ffn_refs, (o_ref, ys_ref) = rest[:-2], rest[-2:]
    tl = x_ref.shape[0]
    for gb in range(S5_BLOCKS):
        for s in range(S5_CHUNK):
            ys_ref[gb, pl.ds(s, tl // S5_CHUNK, stride=S5_CHUNK), :] = (
                y8_ref[gb, :, s * V7X_LANES:(s + 1) * V7X_LANES])
    ys = jnp.concatenate([ys_ref[gb] for gb in range(S5_BLOCKS)], axis=1)
    y = ys + d_ref[...] * u_ref[...]
    y = 0.5 * y * (1.0 + jnp.tanh(math.sqrt(2.0 / math.pi) * (y + 0.044715 * (y * y * y))))
    gate = jax.nn.sigmoid(_dot(y.astype(BF16), gw_ref[...]) + gb_ref[...])
    y = (y * gate).astype(BF16)
    x = x_ref[...] + _dot(yc_ref[...], wc_ref[...]) + _dot(y, ws_ref[...])
    o_ref[...] = _half_ffn(x, *ffn_refs) if ffn_refs else x


def _mix_out(x, y_conv, y_state, u, d, glu_w, glu_b, w_out, bsz, seq, ffn=None):
    t = x.shape[0]
    nl = seq // ROW_TILE
    row = lambda w: pl.BlockSpec((ROW_TILE, w), lambda b, l: (b * nl + l, 0))
    in_specs = [row(D_MODEL), row(CONV_CH), _chunk_layout_spec(), row(SSM_WIDTH), _resident((1, SSM_WIDTH)),
                _resident((SSM_WIDTH, SSM_WIDTH)), _resident((1, SSM_WIDTH)),
                _resident((CONV_CH, D_MODEL)), _resident((SSM_WIDTH, D_MODEL))]
    args = [x, y_conv, y_state, u, d.reshape(1, SSM_WIDTH), glu_w.astype(BF16), glu_b.reshape(1, SSM_WIDTH),
            w_out[:CONV_CH].astype(BF16), w_out[CONV_CH:].astype(BF16)]
    if ffn is not None:
        g, w1, w3, w2, which = ffn
        in_specs += _ffn_weight_specs(which)
        args += [g.reshape(1, D_MODEL), w1, w3, w2]
    return pl.pallas_call(
        _mix_out_body, grid=(bsz, nl), in_specs=in_specs,
        out_specs=row(D_MODEL), out_shape=jax.ShapeDtypeStruct((t, D_MODEL), F32),
        scratch_shapes=[pltpu.VMEM((S5_BLOCKS, ROW_TILE, V7X_LANES), F32)],
        compiler_params=_params(2, 52 * 1024 * 1024), name="mix_out")(*args)


def _moba_body(q_ref, k_ref, v_ref, km_ref, o_ref, kaug_ref, vaug_ref):
    seq = q_ref.shape[0]
    nblk = seq // MOBA_BLOCK
    lane = lax.broadcasted_iota(jnp.int32, (seq, HEAD_DIM), 1)
    blk_of_row = lax.broadcasted_iota(jnp.int32, (seq, HEAD_DIM), 0) // MOBA_BLOCK
    block_one_hot = jnp.where(lane == blk_of_row, 1.0, 0.0).astype(BF16)
    for hh in range(MOBA_HEADS_PER_STEP):
        head = slice(hh * HEAD_DIM, (hh + 1) * HEAD_DIM)
        kaug_ref[hh, :, 0:HEAD_DIM] = k_ref[:, head]
        kaug_ref[hh, :, HEAD_DIM:] = block_one_hot
        vaug_ref[hh, :, 0:HEAD_DIM] = v_ref[:, head]
        vaug_ref[hh, :, HEAD_DIM:] = jnp.ones((seq, HEAD_DIM), BF16)
    blk_id = lax.broadcasted_iota(jnp.int32, (nblk, MOBA_BLOCK), 0)
    qi = lax.broadcasted_iota(jnp.int32, (MOBA_BLOCK, MOBA_BLOCK), 0)
    ki = lax.broadcasted_iota(jnp.int32, (MOBA_BLOCK, MOBA_BLOCK), 1)
    for i, hh in [(i, hh) for i in range(nblk) for hh in range(MOBA_HEADS_PER_STEP)]:
        head = slice(hh * HEAD_DIM, (hh + 1) * HEAD_DIM)
        kaug, vaug = kaug_ref.at[hh], vaug_ref.at[hh]
        own = slice(i * MOBA_BLOCK, (i + 1) * MOBA_BLOCK)
        seen = slice(0, (i + 1) * MOBA_BLOCK)
        q = q_ref[own, head]
        if i > MOBA_TOPK:
            gate = _dot_nt(km_ref[:, head].astype(BF16), q)
            rank = jnp.zeros((nblk, MOBA_BLOCK), jnp.int32)
            for j in range(i):
                gj = gate[j:j + 1, :]
                ahead = jnp.where(gj > gate, 1, jnp.where(gj == gate, jnp.where(blk_id > j, 1, 0), 0))
                rank = rank + ahead
            chosen = jnp.where(blk_id < i, rank, jnp.where(blk_id == i, 0, MOBA_TOPK)) < MOBA_TOPK
        else:
            chosen = blk_id <= i
        bias = jnp.where(chosen, 0.0, MASK_VALUE)
        bias = jnp.concatenate([bias, jnp.zeros((HEAD_DIM - nblk, MOBA_BLOCK), F32)], axis=0)
        q_aug = jnp.concatenate([q, bias.T.astype(BF16)], axis=1)
        s = _dot_nt(q_aug, kaug[seen, :])
        s_own = jnp.where(ki <= qi, s[:, i * MOBA_BLOCK:], MASK_VALUE)
        s = jnp.concatenate([s[:, :i * MOBA_BLOCK], s_own], axis=1) if i else s_own
        m = jnp.max(s, axis=1, keepdims=True)
        p = jnp.exp2((s - m) * SOFTMAX_EXP2_SCALE).astype(BF16)
        o = _dot(p, vaug[seen, :])
        o_ref[own, head] = (o[:, :HEAD_DIM] / o[:, HEAD_DIM:]).astype(BF16)


def _moba(qkv, kmean, bsz, seq):
    nblk = seq // MOBA_BLOCK
    width = MOBA_HEADS_PER_STEP * HEAD_DIM
    steps = N_HEADS // MOBA_HEADS_PER_STEP
    col = lambda off: pl.BlockSpec((seq, width), lambda b, h: (b, off + h))
    aug = pltpu.VMEM((MOBA_HEADS_PER_STEP, seq, 2 * HEAD_DIM), BF16)
    return pl.pallas_call(
        _moba_body, grid=(bsz, steps),
        in_specs=[col(0), col(steps), col(2 * steps), pl.BlockSpec((nblk, width), lambda b, h: (b, h))],
        out_specs=col(0), out_shape=jax.ShapeDtypeStruct((bsz * seq, D_MODEL), BF16),
        scratch_shapes=[aug, aug],
        compiler_params=_params(2, 40 * 1024 * 1024), name="moba")(qkv, qkv, qkv, kmean)


def kernel(x, ffn_norm, ffn_w1, ffn_w3, ffn_w2, mix_norm, ab_w_in, conv_w, conv_b, conv_ln_g, conv_ln_b, ssm_a_re, ssm_a_im, ssm_b_re, ssm_b_im, ssm_c_re, ssm_c_im, ssm_d, ssm_log_dt, ssm_glu_w, ssm_glu_b, ab_w_out, attn_w_qkv, attn_w_o, final_norm):
    bsz, seq, d = x.shape
    depth = ffn_norm.shape[0]
    assert d == D_MODEL and seq % ROW_TILE == 0 and ROW_TILE % MOBA_BLOCK == 0
    assert bsz % (2 * V7X_SUBLANES) == 0, "the S5 boundary scan walks whole packed bf16 tiles of batch rows"
    x = x.reshape(bsz * seq, d)
    w1, w3, w2 = ffn_w1.astype(BF16), ffn_w3.astype(BF16), ffn_w2.astype(BF16)
    for l in range(depth):
        last = l == depth - 1
        if l % 2:
            x, qkv, kmean = _ffn(x, ffn_norm[l, 0], w1, w3, w2, (l, 0), qkv=(mix_norm[l], attn_w_qkv[l // 2]))
        else:
            x = _ffn(x, ffn_norm[l, 0], w1, w3, w2, (l, 0))
        if l % 2 == 0:
            e = l // 2
            y_conv, u, u8 = _mix_in(x, mix_norm[l], ab_w_in[e], conv_w[e], conv_b[e], conv_ln_g[e],
                                    conv_ln_b[e], bsz, seq)
            tables = _s5_tables(ssm_a_re[e], ssm_a_im[e], ssm_b_re[e], ssm_b_im[e], ssm_c_re[e],
                                ssm_c_im[e], ssm_log_dt[e])
            nchunk = seq // S5_CHUNK
            y8 = _s5(u8.reshape(S5_BLOCKS, nchunk * bsz, S5_WIDE), tables, bsz, seq)
            x = _mix_out(x, y_conv, y8.reshape(S5_BLOCKS, nchunk, bsz * S5_WIDE), u, ssm_d[e], ssm_glu_w[e],
                         ssm_glu_b[e], ab_w_out[e], bsz, seq,
                         ffn=None if last else (ffn_norm[l, 1], w1, w3, w2, (l, 1)))
            if not last:
                continue
            attn = None
        else:
            attn = (_moba(qkv, kmean.reshape(-1, D_MODEL), bsz, seq), attn_w_o[l // 2])
        x = _ffn(x, ffn_norm[l, 1], w1, w3, w2, (l, 1), final_g=final_norm if last else None, attn=attn)
    return x.reshape(bsz, seq, d)
```
